```python
import jax, jax.numpy as jnp
from jax import lax
import numpy as np

D_MODEL = 1024
BATCH = 8
SEQ = 4096
DEPTH = 4

D_MIX = D_MODEL
A_WIDTH = D_MIX // 2
A_HEADS = 4
A_HEAD_DIM = A_WIDTH // A_HEADS
A_CHUNK = 128
B_WIDTH = D_MIX - A_WIDTH
B_HEADS = 4
B_V_DIM = B_WIDTH // B_HEADS
B_K_DIM = B_V_DIM // 2
B_KEY_WIDTH = B_HEADS * B_K_DIM
B_GATE_RANK = 16
B_GATE_TAU = 16.0
B_CHUNK = 64
IN_WIDTHS = (A_WIDTH, A_WIDTH, B_KEY_WIDTH, B_KEY_WIDTH, B_WIDTH, B_WIDTH, B_GATE_RANK)
IN_OFFSETS = tuple(int(o) for o in np.cumsum((0,) + IN_WIDTHS))
P_IN = IN_OFFSETS[-1]
N_GROUPS = 4
EXPERTS_PER_GROUP = 8
N_EXPERTS = N_GROUPS * EXPERTS_PER_GROUP
TOP_K_INNER = 2
D_EXPERT = 512
MOE_BLOCK = 256
EPS = 1e-6

kernel_name = "hymba_sgu_gla_hmoe_adaln_trunk"


def rms_norm(x, gain):
    x32 = x.astype(jnp.float32)
    y = x32 * lax.rsqrt(jnp.mean(x32 * x32, axis=-1, keepdims=True) + EPS)
    return (y * gain.astype(jnp.float32)).astype(x.dtype)


def sgu_mixer(u, v, v_gain, w_s, b_s):
    bsz, s, _ = u.shape
    nc = s // A_CHUNK
    v = rms_norm(v, v_gain).reshape(bsz, nc, A_CHUNK, A_HEADS, A_HEAD_DIM)
    causal = jnp.tril(jnp.ones((A_CHUNK, A_CHUNK), dtype=bool))
    w = jnp.where(causal[None], w_s, 0.0).astype(v.dtype)
    z = jnp.einsum('hts,bcshd->bcthd', w, v) + b_s.T.astype(v.dtype)[:, :, None]
    return u * z.reshape(bsz, s, A_WIDTH)


def gla_mixer(q, k, v, r, a_low, w_alpha, b_alpha, out_gain):
    f32 = jnp.float32
    bsz, s, _ = q.shape
    nc = s // B_CHUNK
    q = q.astype(f32).reshape(bsz, s, B_HEADS, B_K_DIM) * (B_K_DIM ** -0.5)
    k = k.astype(f32).reshape(bsz, s, B_HEADS, B_K_DIM)
    v = v.astype(f32).reshape(bsz, s, B_HEADS, B_V_DIM)
    log_a = jax.nn.log_sigmoid((a_low @ w_alpha + b_alpha).astype(f32)) / B_GATE_TAU
    log_a = log_a.reshape(bsz, s, B_HEADS, B_K_DIM)

    def chunks(t):
        return t.reshape(bsz, nc, B_CHUNK, *t.shape[2:]).swapaxes(0, 1)

    causal = jnp.tril(jnp.ones((B_CHUNK, B_CHUNK), dtype=bool))[None, :, :, None, None]

    def step(state, inp):
        qc, kc, vc, gc = inp
        bcum = jnp.cumsum(gc, axis=1)
        blast = bcum[:, -1]
        o_inter = jnp.einsum('bthk,bhkv->bthv', qc * jnp.exp(bcum), state)
        diff = jnp.where(causal, bcum[:, :, None] - bcum[:, None, :], -jnp.inf)
        scores = jnp.einsum('bthk,bshk,btshk->btsh', qc, kc, jnp.exp(diff))
        o_intra = jnp.einsum('btsh,bshv->bthv', scores, vc)
        k_dec = kc * jnp.exp(blast[:, None] - bcum)
        state = state * jnp.exp(blast)[..., None] + jnp.einsum('bshk,bshv->bhkv', k_dec, vc)
        return state, o_inter + o_intra

    state0 = jnp.zeros((bsz, B_HEADS, B_K_DIM, B_V_DIM), f32)
    _, o = lax.scan(step, state0, (chunks(q), chunks(k), chunks(v), chunks(log_a)))
    o = o.swapaxes(0, 1).reshape(bsz, s, B_HEADS, B_V_DIM)
    o = o * lax.rsqrt(jnp.mean(o * o, axis=-1, keepdims=True) + EPS)
    o = o * out_gain.astype(f32).reshape(B_HEADS, B_V_DIM)
    o = o.reshape(bsz, s, B_WIDTH) * jax.nn.silu(r.astype(f32))
    return o.astype(r.dtype)


def hier_moe(h, w_group, b_group, w_router, b_router, w1, w3, w2):
    bsz, s, d = h.shape
    t = bsz * s
    ht = h.reshape(t, d)
    g_logits = (ht @ w_group + b_group).astype(jnp.float32)
    g_prob = jax.nn.softmax(g_logits, axis=-1)
    grp = jnp.argmax(g_logits, axis=-1)
    p_grp = jnp.take_along_axis(g_prob, grp[:, None], axis=1)[:, 0]
    e_logits = (ht @ w_router + b_router).astype(jnp.float32).reshape(t, N_GROUPS, EXPERTS_PER_GROUP)
    e_sel = jnp.take_along_axis(e_logits, grp[:, None, None], axis=1)[:, 0]
    top_v, top_i = lax.top_k(e_sel, TOP_K_INNER)
    gate = jax.nn.softmax(top_v, axis=-1) * p_grp[:, None]
    expert = grp[:, None] * EXPERTS_PER_GROUP + top_i

    n_assign = t * TOP_K_INNER
    e_flat = expert.reshape(-1)
    tok = jnp.repeat(jnp.arange(t, dtype=jnp.int32), TOP_K_INNER)
    wt = gate.reshape(-1)
    order = jnp.argsort(e_flat)
    e_sorted = e_flat[order]
    tok_sorted = tok[order]
    counts = jax.ops.segment_sum(jnp.ones((n_assign,), jnp.int32), e_flat, num_segments=N_EXPERTS)
    starts = jnp.cumsum(counts) - counts
    padded = ((counts + MOE_BLOCK - 1) // MOE_BLOCK) * MOE_BLOCK
    pad_ends = jnp.cumsum(padded)
    pad_starts = pad_ends - padded
    dest = pad_starts[e_sorted] + (jnp.arange(n_assign, dtype=jnp.int32) - starts[e_sorted])
    n_blocks = (n_assign + MOE_BLOCK - 1) // MOE_BLOCK + N_EXPERTS
    total = n_blocks * MOE_BLOCK
    x_pad = jnp.zeros((total, d), h.dtype).at[dest].set(ht[tok_sorted])
    block_start = jnp.arange(n_blocks, dtype=jnp.int32) * MOE_BLOCK
    block_expert = jnp.minimum(jnp.searchsorted(pad_ends, block_start, side='right'), N_EXPERTS - 1)

    def run_block(args):
        xb, e = args
        return (jax.nn.silu(xb @ w1[e]) * (xb @ w3[e])) @ w2[e]

    y_pad = lax.map(run_block, (x_pad.reshape(n_blocks, MOE_BLOCK, d), block_expert))
    y = y_pad.reshape(total, d)[dest] * wt[order][:, None].astype(h.dtype)
    out = jnp.zeros((t, d), h.dtype).at[tok_sorted].add(y.astype(h.dtype))
    return out.reshape(bsz, s, d)


def setup_inputs(seed: int = 0) -> dict:
    key = jax.random.key(seed)
    ks = jax.random.split(key, 24)
    f32 = jnp.float32
    nrm = lambda k, shape, scale: jax.random.normal(k, shape, f32) * scale
    return {
        "x": nrm(ks[0], (BATCH, SEQ, D_MODEL), 1.0),
        "c": nrm(ks[1], (BATCH, D_MODEL), 1.0),
        "w_ada": nrm(ks[2], (DEPTH, D_MODEL, 6 * D_MODEL), 0.5 * D_MODEL ** -0.5),
        "b_ada": nrm(ks[3], (DEPTH, 6 * D_MODEL), 0.02),
        "norm1_gain": 1.0 + nrm(ks[4], (DEPTH, D_MODEL), 0.02),
        "norm2_gain": 1.0 + nrm(ks[5], (DEPTH, D_MODEL), 0.02),
        "w_in": nrm(ks[6], (DEPTH, D_MODEL, P_IN), D_MODEL ** -0.5),
        "a_v_gain": 1.0 + nrm(ks[7], (DEPTH, A_WIDTH), 0.02),
        "a_w_s": nrm(ks[8], (DEPTH, A_HEADS, A_CHUNK, A_CHUNK), A_CHUNK ** -0.5),
        "a_b_s": 1.0 + nrm(ks[9], (DEPTH, A_HEADS, A_CHUNK), 0.02),
        "b_w_alpha": nrm(ks[10], (DEPTH, B_GATE_RANK, B_KEY_WIDTH), B_GATE_RANK ** -0.5),
        "b_b_alpha": nrm(ks[11], (DEPTH, B_KEY_WIDTH), 0.1),
        "b_out_gain": 1.0 + nrm(ks[12], (DEPTH, B_WIDTH), 0.02),
        "w_out": nrm(ks[13], (DEPTH, D_MIX, D_MODEL), D_MIX ** -0.5),
        "w_group": nrm(ks[14], (DEPTH, D_MODEL, N_GROUPS), D_MODEL ** -0.5),
        "b_group": nrm(ks[15], (DEPTH, N_GROUPS), 0.01),
        "w_router": nrm(ks[16], (DEPTH, D_MODEL, N_EXPERTS), D_MODEL ** -0.5),
        "b_router": nrm(ks[17], (DEPTH, N_EXPERTS), 0.01),
        "w1": nrm(ks[18], (DEPTH, N_EXPERTS, D_MODEL, D_EXPERT), D_MODEL ** -0.5),
        "w3": nrm(ks[19], (DEPTH, N_EXPERTS, D_MODEL, D_EXPERT), D_MODEL ** -0.5),
        "w2": nrm(ks[20], (DEPTH, N_EXPERTS, D_EXPERT, D_MODEL), D_EXPERT ** -0.5),
        "final_gain": 1.0 + nrm(ks[21], (D_MODEL,), 0.02),
    }


def reference(x, c, w_ada, b_ada, norm1_gain, norm2_gain, w_in, a_v_gain, a_w_s, a_b_s,
              b_w_alpha, b_b_alpha, b_out_gain, w_out, w_group, b_group, w_router, b_router,
              w1, w3, w2, final_gain):
    o = IN_OFFSETS
    c_act = jax.nn.silu(c)
    for l in range(DEPTH):
        mod = c_act @ w_ada[l] + b_ada[l]
        sh1, sc1, g1, sh2, sc2, g2 = [m[:, None, :] for m in jnp.split(mod, 6, axis=-1)]
        h = rms_norm(x, norm1_gain[l]) * (1.0 + sc1) + sh1
        proj = h @ w_in[l]
        a_u = jax.nn.gelu(proj[..., o[0]:o[1]])
        a_v = jax.nn.gelu(proj[..., o[1]:o[2]])
        y_a = sgu_mixer(a_u, a_v, a_v_gain[l], a_w_s[l], a_b_s[l])
        y_b = gla_mixer(proj[..., o[2]:o[3]], proj[..., o[3]:o[4]], proj[..., o[4]:o[5]],
                        proj[..., o[5]:o[6]], proj[..., o[6]:o[7]],
                        b_w_alpha[l], b_b_alpha[l], b_out_gain[l])
        mixed = jnp.concatenate([y_a, y_b], axis=-1) @ w_out[l]
        x = x + g1 * mixed
        h2 = rms_norm(x, norm2_gain[l]) * (1.0 + sc2) + sh2
        x = x + g2 * hier_moe(h2, w_group[l], b_group[l], w_router[l], b_router[l],
                              w1[l], w3[l], w2[l])
    return rms_norm(x, final_gain)
```

```python
import functools

import jax
import jax.numpy as jnp
from jax import lax
from jax.experimental import pallas as pl
from jax.experimental.pallas import tpu as pltpu

F32 = jnp.float32
BF16 = jnp.bfloat16
HIGHEST = lax.Precision.HIGHEST

EPS = 1e-6
LANES = 128
SUBLANES = 8
A_HEADS = 4
A_CHUNK = 128
B_HEADS = 4
B_GATE_RANK = 16
B_GATE_TAU = 16.0
GLA_CHUNK = 128
N_GROUPS = 4
EXPERTS_PER_GROUP = 8
N_EXPERTS = N_GROUPS * EXPERTS_PER_GROUP
MOE_BLOCK = 256
VMEM_LIMIT = 56 * 1024 * 1024


def _nt(a, b):
    return lax.dot_general(a, b, (((1,), (1,)), ((), ())), preferred_element_type=F32)


def _tn(a, b):
    return lax.dot_general(a, b, (((0,), (0,)), ((), ())), preferred_element_type=F32)


def _rows_to_tiles(ref, val, n_rows):
    for c in range(SUBLANES):
        ref[pl.ds(c, n_rows, stride=SUBLANES), :] = val[:, c * LANES:(c + 1) * LANES]


def _tiles_to_rows(ref, n_rows):
    return jnp.concatenate(
        [ref[pl.ds(c, n_rows, stride=SUBLANES), :] for c in range(SUBLANES)], axis=-1)


def _ada_kernel(c_ref, w_ref, b_ref, o_ref):
    c = c_ref[...]
    c_act = c * jax.nn.sigmoid(c)
    o_ref[0] = jnp.dot(c_act, w_ref[0], precision=HIGHEST, preferred_element_type=F32) + b_ref[0]


def _ada(c, w_ada, b_ada, tn=1024):
    depth, d, n = w_ada.shape
    bsz = c.shape[0]
    return pl.pallas_call(
        _ada_kernel,
        grid=(depth, n // tn),
        in_specs=[
            pl.BlockSpec((bsz, d), lambda l, j: (0, 0)),
            pl.BlockSpec((1, d, tn), lambda l, j: (l, 0, j)),
            pl.BlockSpec((1, 1, tn), lambda l, j: (l, 0, j)),
        ],
        out_specs=pl.BlockSpec((1, bsz, tn), lambda l, j: (l, 0, j)),
        out_shape=jax.ShapeDtypeStruct((depth, bsz, n), F32),
        compiler_params=pltpu.CompilerParams(
            dimension_semantics=("arbitrary", "arbitrary"), vmem_limit_bytes=VMEM_LIMIT),
        name="ada_mod",
    )(c, w_ada, b_ada.reshape(depth, 1, n))


def _mixer_kernel(x_ref, mod_ref, g1_ref, win_ref, avg_ref, ws_ref, bsb_ref, wal_ref, bal_ref,
                  og_ref, wout_ref, g2_ref, wr_ref, br_ref,
                  xo_ref, h2_ref, route_ref, cnt_ref,
                  st_scr, proj_scr, la_scr, o_scr, cnt_scr, *, ts, d):
    a_w = d // 2
    kw = d // 4
    vw = d // 2
    hk = kw // B_HEADS
    hv = vw // B_HEADS
    o_q, o_k, o_v, o_r, o_a = 2 * a_w, 2 * a_w + kw, 2 * a_w + 2 * kw, 2 * a_w + 2 * kw + vw, \
        2 * a_w + 2 * kw + 2 * vw

    b = pl.program_id(0)
    s = pl.program_id(1)

    @pl.when(s == 0)
    def _():
        st_scr[...] = jnp.zeros_like(st_scr)

    @pl.when(jnp.logical_and(b == 0, s == 0))
    def _():
        cnt_scr[...] = jnp.zeros_like(cnt_scr)

    x = x_ref[...]
    mod = mod_ref[0]
    sh1, sc1, g1 = mod[0:1], mod[1:2], mod[2:3]
    sh2, sc2, g2 = mod[3:4], mod[4:5], mod[5:6]

    h = x * lax.rsqrt(jnp.mean(x * x, axis=-1, keepdims=True) + EPS) * g1_ref[...]
    h = h * (1.0 + sc1) + sh1
    proj_scr[...] = jnp.dot(h.astype(BF16), win_ref[...], preferred_element_type=F32)

    u = jax.nn.gelu(proj_scr[:, 0:a_w])
    v = jax.nn.gelu(proj_scr[:, a_w:2 * a_w])
    vn = v * lax.rsqrt(jnp.mean(v * v, axis=-1, keepdims=True) + EPS) * avg_ref[...]
    vn = vn.astype(BF16)
    ahd = a_w // A_HEADS
    z_rows = []
    for ci in range(ts // A_CHUNK):
        z_heads = []
        for hd in range(A_HEADS):
            vc = vn[ci * A_CHUNK:(ci + 1) * A_CHUNK, hd * ahd:(hd + 1) * ahd]
            z_heads.append(jnp.dot(ws_ref[hd], vc, preferred_element_type=F32) + bsb_ref[hd])
        z_rows.append(jnp.concatenate(z_heads, axis=-1))
    y_a = u * jnp.concatenate(z_rows, axis=0)

    xg = jnp.dot(proj_scr[:, o_a:o_a + LANES], wal_ref[...], precision=HIGHEST,
                 preferred_element_type=F32) + bal_ref[...]
    la_scr[...] = (jnp.minimum(xg, 0.0) - jnp.log(1.0 + jnp.exp(-jnp.abs(xg)))) / B_GATE_TAU

    c = GLA_CHUNK
    row = lax.broadcasted_iota(jnp.int32, (c, c), 0)
    col = lax.broadcasted_iota(jnp.int32, (c, c), 1)
    causal = col <= row
    tri = causal.astype(F32)
    klane = lax.broadcasted_iota(jnp.int32, (1, kw), 1)
    bd_r = lax.broadcasted_iota(jnp.int32, (vw, kw), 0) // hv
    bd_c = lax.broadcasted_iota(jnp.int32, (vw, kw), 1) // hk
    bd_mask = (bd_r == bd_c).astype(F32)
    mid = c // 2 - 1

    def gla_chunk(j, carry):
        r0 = pl.multiple_of(j * c, c)
        g = la_scr[pl.ds(r0, c), :]
        q = proj_scr[pl.ds(r0, c), o_q:o_q + kw] * (hk ** -0.5)
        k = proj_scr[pl.ds(r0, c), o_k:o_k + kw]
        vv = proj_scr[pl.ds(r0, c), o_v:o_v + vw].astype(BF16)
        bcum = jnp.dot(tri, g, precision=HIGHEST, preferred_element_type=F32)
        blast = bcum[c - 1:c, :]
        bmid = bcum[mid:mid + 1, :]
        qe = (q * jnp.exp(bcum)).astype(BF16)
        qt = q * jnp.exp(bcum - bmid)
        kt = (k * jnp.exp(bmid - bcum)).astype(BF16)
        kd = (k * jnp.exp(blast - bcum)).astype(BF16)
        st = st_scr[...]
        o_inter = _nt(qe, st.astype(BF16))
        o_heads = []
        for hd in range(B_HEADS):
            m_h = jnp.logical_and(klane >= hd * hk, klane < (hd + 1) * hk)
            a_h = _nt(jnp.where(m_h, qt, 0.0).astype(BF16), kt)
            a_h = jnp.where(causal, a_h, 0.0).astype(BF16)
            o_heads.append(jnp.dot(a_h, vv[:, hd * hv:(hd + 1) * hv], preferred_element_type=F32))
        o_scr[pl.ds(r0, c), :] = o_inter + jnp.concatenate(o_heads, axis=-1)
        st_scr[...] = st * jnp.exp(blast) + _tn(vv, kd) * bd_mask
        return carry

    lax.fori_loop(0, ts // c, gla_chunk, 0)

    o = o_scr[...]
    og = og_ref[...]
    yb_heads = []
    for hd in range(B_HEADS):
        oh = o[:, hd * hv:(hd + 1) * hv]
        oh = oh * lax.rsqrt(jnp.mean(oh * oh, axis=-1, keepdims=True) + EPS)
        yb_heads.append(oh * og[:, hd * hv:(hd + 1) * hv])
    r = proj_scr[:, o_r:o_r + vw]
    y_b = jnp.concatenate(yb_heads, axis=-1) * (r * jax.nn.sigmoid(r))

    mixed = jnp.dot(jnp.concatenate([y_a, y_b], axis=-1).astype(BF16), wout_ref[...],
                    preferred_element_type=F32)
    xn = x + g1 * mixed
    xo_ref[...] = xn

    h2 = xn * lax.rsqrt(jnp.mean(xn * xn, axis=-1, keepdims=True) + EPS) * g2_ref[...]
    h2 = h2 * (1.0 + sc2) + sh2
    _rows_to_tiles(h2_ref, h2, ts)

    logits = jnp.dot(h2, wr_ref[...], precision=HIGHEST, preferred_element_type=F32) + br_ref[...]
    lane = lax.broadcasted_iota(jnp.int32, (ts, LANES), 1).astype(F32)
    neg = -jnp.inf
    big = float(LANES)
    is_g = lane < N_GROUPS
    gl = jnp.where(is_g, logits, neg)
    gmax = jnp.max(gl, axis=-1, keepdims=True)
    grp = jnp.min(jnp.where(gl == gmax, lane, big), axis=-1, keepdims=True)
    gsum = jnp.sum(jnp.where(is_g, jnp.exp(logits - gmax), 0.0), axis=-1, keepdims=True)
    p_grp = 1.0 / gsum
    lo = N_GROUPS + EXPERTS_PER_GROUP * grp
    el = jnp.where(jnp.logical_and(lane >= lo, lane < lo + EXPERTS_PER_GROUP), logits, neg)
    t1 = jnp.max(el, axis=-1, keepdims=True)
    i1 = jnp.min(jnp.where(el == t1, lane, big), axis=-1, keepdims=True)
    el2 = jnp.where(lane == i1, neg, el)
    t2 = jnp.max(el2, axis=-1, keepdims=True)
    i2 = jnp.min(jnp.where(el2 == t2, lane, big), axis=-1, keepdims=True)
    e2 = jnp.exp(t2 - t1)
    w0 = (1.0 / (1.0 + e2)) * p_grp
    w1 = (e2 / (1.0 + e2)) * p_grp
    e0 = i1 - N_GROUPS
    e1 = i2 - N_GROUPS

    sel0 = lane == e0
    sel1 = lane == e1
    onehot = jnp.logical_or(sel0, sel1)
    rr = lax.broadcasted_iota(jnp.int32, (ts, ts), 0)
    cc = lax.broadcasted_iota(jnp.int32, (ts, ts), 1)
    before = (cc < rr).astype(BF16)
    prefix = jnp.dot(before, onehot.astype(BF16), preferred_element_type=F32) + cnt_scr[0:1, :]
    rank0 = jnp.sum(jnp.where(sel0, prefix, 0.0), axis=-1, keepdims=True)
    rank1 = jnp.sum(jnp.where(sel1, prefix, 0.0), axis=-1, keepdims=True)
    cnt_new = cnt_scr[0:1, :] + jnp.sum(onehot.astype(F32), axis=0, keepdims=True)
    cnt_scr[...] = jnp.broadcast_to(cnt_new, cnt_scr.shape)
    cnt_ref[...] = jnp.broadcast_to(cnt_new, cnt_ref.shape)

    route = jnp.where(lane == 0, e0, 0.0)
    route = jnp.where(lane == 1, e1, route)
    route = jnp.where(lane == 2, w0, route)
    route = jnp.where(lane == 3, w1, route)
    route = jnp.where(lane == 4, rank0, route)
    route = jnp.where(lane == 5, rank1, route)
    route_ref[...] = route


def _mixer(x2, mod_l, g1, win, avg, ws, bsb, wal, bal, og, wout, g2, wr, br, *, bsz, seq, ts):
    t, d = x2.shape
    n_s = seq // ts
    p_in = win.shape[1]
    kw, vw = d // 4, d // 2
    const2 = lambda b, s: (0, 0)
    const3 = lambda b, s: (0, 0, 0)
    tok = lambda b, s: (b * n_s + s, 0)
    kern = functools.partial(_mixer_kernel, ts=ts, d=d)
    return pl.pallas_call(
        kern,
        grid=(bsz, n_s),
        in_specs=[
            pl.BlockSpec((ts, d), tok),
            pl.BlockSpec((1, 6, d), lambda b, s: (b, 0, 0)),
            pl.BlockSpec((1, d), const2),
            pl.BlockSpec((d, p_in), const2),
            pl.BlockSpec((1, d // 2), const2),
            pl.BlockSpec((A_HEADS, A_CHUNK, A_CHUNK), const3),
            pl.BlockSpec((A_HEADS, A_CHUNK, A_CHUNK), const3),
            pl.BlockSpec((LANES, kw), const2),
            pl.BlockSpec((1, kw), const2),
            pl.BlockSpec((1, vw), const2),
            pl.BlockSpec((d, d), const2),
            pl.BlockSpec((1, d), const2),
            pl.BlockSpec((d, LANES), const2),
            pl.BlockSpec((1, LANES), const2),
        ],
        out_specs=[
            pl.BlockSpec((ts, d), tok),
            pl.BlockSpec((ts * SUBLANES, LANES), tok),
            pl.BlockSpec((ts, LANES), tok),
            pl.BlockSpec((SUBLANES, LANES), const2),
        ],
        out_shape=[
            jax.ShapeDtypeStruct((t, d), F32),
            jax.ShapeDtypeStruct((t * SUBLANES, LANES), F32),
            jax.ShapeDtypeStruct((t, LANES), F32),
            jax.ShapeDtypeStruct((SUBLANES, LANES), F32),
        ],
        scratch_shapes=[
            pltpu.VMEM((vw, kw), F32),
            pltpu.VMEM((ts, p_in), F32),
            pltpu.VMEM((ts, kw), F32),
            pltpu.VMEM((ts, vw), F32),
            pltpu.VMEM((SUBLANES, LANES), F32),
        ],
        compiler_params=pltpu.CompilerParams(
            dimension_semantics=("arbitrary", "arbitrary"), vmem_limit_bytes=VMEM_LIMIT),
        name="mixer",
    )(x2, mod_l, g1, win, avg, ws, bsb, wal, bal, og, wout, g2, wr, br)


def _dispatch_kernel(d0_ref, d1_ref, h2_ref, xpad_in_ref, xpad_ref, sem0, sem1, *, td, unroll):
    del xpad_in_ref
    i = pl.program_id(0)
    base = i * td

    def row_copy(r, dst, sem):
        return pltpu.make_async_copy(
            h2_ref.at[pl.ds(pl.multiple_of(r * SUBLANES, SUBLANES), SUBLANES)],
            xpad_ref.at[pl.ds(pl.multiple_of(dst * SUBLANES, SUBLANES), SUBLANES)], sem)

    def issue(jj, carry):
        for uu in range(unroll):
            r = jj * unroll + uu
            row_copy(r, d0_ref[base + r], sem0).start()
            row_copy(r, d1_ref[base + r], sem1).start()
        return carry

    lax.fori_loop(0, td // unroll, issue, 0)

    def drain(jj, carry):
        for uu in range(unroll):
            r = jj * unroll + uu
            row_copy(r, d0_ref[base + r], sem0).wait()
            row_copy(r, d1_ref[base + r], sem1).wait()
        return carry

    lax.fori_loop(0, td // unroll, drain, 0)


def _dispatch(d0, d1, h2t, xpad0, *, td=512, unroll=8):
    t = d0.shape[0]
    kern = functools.partial(_dispatch_kernel, td=td, unroll=unroll)
    return pl.pallas_call(
        kern,
        grid_spec=pltpu.PrefetchScalarGridSpec(
            num_scalar_prefetch=2,
            grid=(t // td,),
            in_specs=[
                pl.BlockSpec((td * SUBLANES, LANES), lambda i, d0, d1: (i, 0)),
                pl.BlockSpec(memory_space=pl.ANY),
            ],
            out_specs=pl.BlockSpec(memory_space=pl.ANY),
            scratch_shapes=[pltpu.SemaphoreType.DMA, pltpu.SemaphoreType.DMA],
        ),
        out_shape=jax.ShapeDtypeStruct(xpad0.shape, F32),
        input_output_aliases={3: 0},
        compiler_params=pltpu.CompilerParams(
            dimension_semantics=("arbitrary",), vmem_limit_bytes=VMEM_LIMIT),
        name="dispatch",
    )(d0, d1, h2t, xpad0)


def _experts_kernel(be_ref, nu_ref, x_ref, w1_ref, w3_ref, w2_ref, y_ref, w1b, w3b, w2b, *, blk):
    j = pl.program_id(0)
    used = j < nu_ref[0]
    prev = be_ref[jnp.maximum(j - 1, 0)]
    fresh = jnp.logical_or(j == 0, be_ref[j] != prev)

    @pl.when(jnp.logical_and(used, fresh))
    def _():
        w1b[...] = w1_ref[0].astype(BF16)
        w3b[...] = w3_ref[0].astype(BF16)
        w2b[...] = w2_ref[0].astype(BF16)

    @pl.when(used)
    def _():
        x = _tiles_to_rows(x_ref, blk).astype(BF16)
        a = jnp.dot(x, w1b[...], preferred_element_type=F32)
        g = jnp.dot(x, w3b[...], preferred_element_type=F32)
        hmid = ((a * jax.nn.sigmoid(a)) * g).astype(BF16)
        y = jnp.dot(hmid, w2b[...], preferred_element_type=F32)
        _rows_to_tiles(y_ref, y, blk)

    @pl.when(jnp.logical_not(used))
    def _():
        y_ref[...] = jnp.zeros_like(y_ref)


def _experts(block_expert, n_used, xpad, w1, w3, w2, *, blk=MOE_BLOCK):
    n_blocks = block_expert.shape[0]
    _, d, de = w1.shape
    kern = functools.partial(_experts_kernel, blk=blk)
    return pl.pallas_call(
        kern,
        grid_spec=pltpu.PrefetchScalarGridSpec(
            num_scalar_prefetch=2,
            grid=(n_blocks,),
            in_specs=[
                pl.BlockSpec((blk * SUBLANES, LANES), lambda j, be, nu: (j, 0)),
                pl.BlockSpec((1, d, de), lambda j, be, nu: (be[j], 0, 0)),
                pl.BlockSpec((1, d, de), lambda j, be, nu: (be[j], 0, 0)),
                pl.BlockSpec((1, de, d), lambda j, be, nu: (be[j], 0, 0)),
            ],
            out_specs=pl.BlockSpec((blk * SUBLANES, LANES), lambda j, be, nu: (j, 0)),
            scratch_shapes=[
                pltpu.VMEM((d, de), BF16),
                pltpu.VMEM((d, de), BF16),
                pltpu.VMEM((de, d), BF16),
            ],
        ),
        out_shape=jax.ShapeDtypeStruct(xpad.shape, F32),
        compiler_params=pltpu.CompilerParams(
            dimension_semantics=("arbitrary",), vmem_limit_bytes=VMEM_LIMIT),
        name="experts",
    )(block_expert, n_used, xpad, w1, w3, w2)


def _combine_kernel(d0_ref, d1_ref, x_ref, route_ref, mod_ref, fg_ref, ypad_ref, o_ref,
                    buf0, buf1, sem0, sem1, *, tc, unroll, final):
    i = pl.program_id(0)
    base = i * tc

    def row_copy(r, src, buf, sem):
        return pltpu.make_async_copy(
            ypad_ref.at[pl.ds(pl.multiple_of(src * SUBLANES, SUBLANES), SUBLANES)],
            buf.at[pl.ds(pl.multiple_of(r * SUBLANES, SUBLANES), SUBLANES)], sem)

    def issue(jj, carry):
        for uu in range(unroll):
            r = jj * unroll + uu
            row_copy(r, d0_ref[base + r], buf0, sem0).start()
            row_copy(r, d1_ref[base + r], buf1, sem1).start()
        return carry

    lax.fori_loop(0, tc // unroll, issue, 0)

    def drain(jj, carry):
        for uu in range(unroll):
            r = jj * unroll + uu
            row_copy(r, d0_ref[base + r], buf0, sem0).wait()
            row_copy(r, d1_ref[base + r], buf1, sem1).wait()
        return carry

    lax.fori_loop(0, tc // unroll, drain, 0)

    route = route_ref[...]
    w0 = route[:, 2:3]
    w1 = route[:, 3:4]
    g2 = mod_ref[0][5:6]
    y0 = _tiles_to_rows(buf0, tc)
    y1 = _tiles_to_rows(buf1, tc)
    xo = x_ref[...] + g2 * (y0 * w0 + y1 * w1)
    if final:
        xo = xo * lax.rsqrt(jnp.mean(xo * xo, axis=-1, keepdims=True) + EPS) * fg_ref[...]
    o_ref[...] = xo


def _combine(d0, d1, x2, route, mod_l, fgain, ypad, *, seq, final, tc=512, unroll=8):
    t, d = x2.shape
    per_b = seq // tc
    kern = functools.partial(_combine_kernel, tc=tc, unroll=unroll, final=final)
    return pl.pallas_call(
        kern,
        grid_spec=pltpu.PrefetchScalarGridSpec(
            num_scalar_prefetch=2,
            grid=(t // tc,),
            in_specs=[
                pl.BlockSpec((tc, d), lambda i, d0, d1: (i, 0)),
                pl.BlockSpec((tc, LANES), lambda i, d0, d1: (i, 0)),
                pl.BlockSpec((1, 6, d), lambda i, d0, d1: (i // per_b, 0, 0)),
                pl.BlockSpec((1, d), lambda i, d0, d1: (0, 0)),
                pl.BlockSpec(memory_space=pl.ANY),
            ],
            out_specs=pl.BlockSpec((tc, d), lambda i, d0, d1: (i, 0)),
            scratch_shapes=[
                pltpu.VMEM((tc * SUBLANES, LANES), F32),
                pltpu.VMEM((tc * SUBLANES, LANES), F32),
                pltpu.SemaphoreType.DMA,
                pltpu.SemaphoreType.DMA,
            ],
        ),
        out_shape=jax.ShapeDtypeStruct((t, d), F32),
        compiler_params=pltpu.CompilerParams(
            dimension_semantics=("arbitrary",), vmem_limit_bytes=VMEM_LIMIT),
        name="combine",
    )(d0, d1, x2, route, mod_l, fgain, ypad)


def _route_tables(route, counts, blk, n_blocks):
    e0 = route[:, 0].astype(jnp.int32)
    e1 = route[:, 1].astype(jnp.int32)
    r0 = route[:, 4].astype(jnp.int32)
    r1 = route[:, 5].astype(jnp.int32)
    cnt = counts[0, :N_EXPERTS].astype(jnp.int32)
    padded = ((cnt + blk - 1) // blk) * blk
    pad_ends = jnp.cumsum(padded)
    pad_starts = pad_ends - padded
    d0 = pad_starts[e0] + r0
    d1 = pad_starts[e1] + r1
    block_start = jnp.arange(n_blocks, dtype=jnp.int32) * blk
    block_expert = jnp.minimum(
        jnp.searchsorted(pad_ends, block_start, side="right"), N_EXPERTS - 1).astype(jnp.int32)
    n_used = (pad_ends[-1:] // blk).astype(jnp.int32)
    return d0, d1, block_expert, n_used


@jax.jit
def _forward(x, c, w_ada, b_ada, norm1_gain, norm2_gain, w_in, a_v_gain, a_w_s, a_b_s, b_w_alpha,
             b_b_alpha, b_out_gain, w_out, w_group, b_group, w_router, b_router, w1, w3, w2,
             final_gain):
    bsz, seq, d = x.shape
    depth = w_ada.shape[0]
    t = bsz * seq
    ts = min(512, seq)
    blk = MOE_BLOCK
    n_blocks = (2 * t + blk - 1) // blk + N_EXPERTS
    p_in = w_in.shape[-1]
    p_main = p_in - B_GATE_RANK
    kw = d // 4

    mod = _ada(c, w_ada, b_ada).reshape(depth, bsz, 6, d)

    win = jnp.concatenate(
        [w_in[..., :p_main],
         jnp.pad(w_in[..., p_main:], ((0, 0), (0, 0), (0, LANES - B_GATE_RANK)))], axis=-1).astype(BF16)
    wal = jnp.pad(b_w_alpha, ((0, 0), (0, LANES - B_GATE_RANK), (0, 0)))
    causal = jnp.tril(jnp.ones((A_CHUNK, A_CHUNK), dtype=bool))
    ws = jnp.where(causal[None, None], a_w_s, 0.0).astype(BF16)
    bsb = jnp.broadcast_to(a_b_s[..., None], a_b_s.shape + (d // 2 // A_HEADS,))
    wout = w_out.astype(BF16)
    n_r = N_GROUPS + N_EXPERTS
    wr = jnp.pad(jnp.concatenate([w_group, w_router], axis=-1), ((0, 0), (0, 0), (0, LANES - n_r)))
    br = jnp.pad(jnp.concatenate([b_group, b_router], axis=-1), ((0, 0), (0, LANES - n_r)))

    x2 = x.reshape(t, d)
    fg = final_gain.reshape(1, d)
    for l in range(depth):
        x2, h2t, route, counts = _mixer(
            x2, mod[l], norm1_gain[l].reshape(1, d), win[l], a_v_gain[l].reshape(1, d // 2),
            ws[l], bsb[l], wal[l], b_b_alpha[l].reshape(1, kw), b_out_gain[l].reshape(1, d // 2),
            wout[l], norm2_gain[l].reshape(1, d), wr[l], br[l].reshape(1, LANES),
            bsz=bsz, seq=seq, ts=ts)
        d0, d1, block_expert, n_used = _route_tables(route, counts, blk, n_blocks)
        xpad = _dispatch(d0, d1, h2t, jnp.zeros((n_blocks * blk * SUBLANES, LANES), F32))
        ypad = _experts(block_expert, n_used, xpad, w1[l], w3[l], w2[l], blk=blk)
        x2 = _combine(d0, d1, x2, route, mod[l], fg, ypad, seq=seq, final=(l == depth - 1))
    return x2.reshape(bsz, seq, d)


def kernel(x, c, w_ada, b_ada, norm1_gain, norm2_gain, w_in, a_v_gain, a_w_s, a_b_s, b_w_alpha,
           b_b_alpha, b_out_gain, w_out, w_group, b_group, w_router, b_router, w1, w3, w2,
           final_gain):
    return _forward(x, c, w_ada, b_ada, norm1_gain, norm2_gain, w_in, a_v_gain, a_w_s, a_b_s,
                    b_w_alpha, b_b_alpha, b_out_gain, w_out, w_group, b_group, w_router, b_router,
                    w1, w3, w2, final_gain)
```

```python
import functools

import jax
import jax.numpy as jnp
from jax import lax
from jax.experimental import pallas as pl
from jax.experimental.pallas import tpu as pltpu

F32 = jnp.float32
BF16 = jnp.bfloat16
I32 = jnp.int32
HIGHEST = lax.Precision.HIGHEST

EPS = 1e-6
LANES = 128
SUBLANES = 8
A_HEADS = 4
A_CHUNK = 128
B_HEADS = 4
B_GATE_RANK = 16
B_GATE_TAU = 16.0
GLA_CHUNK = 128
N_GROUPS = 4
EXPERTS_PER_GROUP = 8
N_EXPERTS = N_GROUPS * EXPERTS_PER_GROUP
MOE_BLOCK = 256
TOKEN_TILE = 512
CODE_SHIFT = 65536.0
VMEM_LIMIT = 56 * 1024 * 1024


def _nt(a, b):
    return lax.dot_general(a, b, (((1,), (1,)), ((), ())), preferred_element_type=F32)


def _tn(a, b):
    return lax.dot_general(a, b, (((0,), (0,)), ((), ())), preferred_element_type=F32)


def _rows_to_tiles(ref, val, n_rows):
    for c in range(SUBLANES):
        ref[pl.ds(c, n_rows, stride=SUBLANES), :] = val[:, c * LANES:(c + 1) * LANES]


def _tiles_to_rows(ref, n_rows):
    return jnp.concatenate(
        [ref[pl.ds(c, n_rows, stride=SUBLANES), :] for c in range(SUBLANES)], axis=-1)


def _col_to_lanes(col, n):
    r = lax.broadcasted_iota(I32, (LANES, LANES), 0)
    c = lax.broadcasted_iota(I32, (LANES, LANES), 1)
    eye = r == c
    rows = [jnp.sum(jnp.where(eye, col[g * LANES:(g + 1) * LANES, :], 0.0), axis=0, keepdims=True)
            for g in range(n // LANES)]
    return jnp.concatenate(rows, axis=0)


def _ada_kernel(c_ref, w_ref, b_ref, o_ref):
    c = c_ref[...]
    c_act = c * jax.nn.sigmoid(c)
    o_ref[0] = jnp.dot(c_act, w_ref[0], precision=HIGHEST, preferred_element_type=F32) + b_ref[0]


def _ada(c, w_ada, b_ada, tn=1024):
    depth, d, n = w_ada.shape
    bsz = c.shape[0]
    return pl.pallas_call(
        _ada_kernel,
        grid=(depth, n // tn),
        in_specs=[
            pl.BlockSpec((bsz, d), lambda l, j: (0, 0)),
            pl.BlockSpec((1, d, tn), lambda l, j: (l, 0, j)),
            pl.BlockSpec((1, 1, tn), lambda l, j: (l, 0, j)),
        ],
        out_specs=pl.BlockSpec((1, bsz, tn), lambda l, j: (l, 0, j)),
        out_shape=jax.ShapeDtypeStruct((depth, bsz, n), F32),
        compiler_params=pltpu.CompilerParams(
            dimension_semantics=("arbitrary", "arbitrary"), vmem_limit_bytes=VMEM_LIMIT),
        name="ada_mod",
    )(c, w_ada, b_ada.reshape(depth, 1, n))


def _mixer_kernel(x_ref, mod_ref, g1_ref, win_ref, avg_ref, ws_ref, bsb_ref, wal_ref, bal_ref,
                  og_ref, wout_ref, g2_ref, wrh_ref, wrl_ref, br_ref,
                  xo_ref, h2_ref, gate_ref, code_ref, cnt_ref,
                  st_scr, proj_scr, la_scr, o_scr, cnt_scr, *, ts, d):
    a_w = d // 2
    kw = d // 4
    vw = d // 2
    hk = kw // B_HEADS
    hv = vw // B_HEADS
    o_q, o_k, o_v, o_r, o_a = 2 * a_w, 2 * a_w + kw, 2 * a_w + 2 * kw, 2 * a_w + 2 * kw + vw, \
        2 * a_w + 2 * kw + 2 * vw

    b = pl.program_id(0)
    s = pl.program_id(1)

    @pl.when(s == 0)
    def _():
        st_scr[...] = jnp.zeros_like(st_scr)

    @pl.when(jnp.logical_and(b == 0, s == 0))
    def _():
        cnt_scr[...] = jnp.zeros_like(cnt_scr)

    x = x_ref[...]
    mod = mod_ref[0]
    sh1, sc1, g1 = mod[0:1], mod[1:2], mod[2:3]
    sh2, sc2 = mod[3:4], mod[4:5]

    h = x * lax.rsqrt(jnp.mean(x * x, axis=-1, keepdims=True) + EPS) * g1_ref[...]
    h = h * (1.0 + sc1) + sh1
    proj_scr[...] = jnp.dot(h.astype(BF16), win_ref[...], preferred_element_type=F32)

    u = jax.nn.gelu(proj_scr[:, 0:a_w])
    v = jax.nn.gelu(proj_scr[:, a_w:2 * a_w])
    vn = v * lax.rsqrt(jnp.mean(v * v, axis=-1, keepdims=True) + EPS) * avg_ref[...]
    vn = vn.astype(BF16)
    ahd = a_w // A_HEADS
    z_rows = []
    for ci in range(ts // A_CHUNK):
        z_heads = []
        for hd in range(A_HEADS):
            vc = vn[ci * A_CHUNK:(ci + 1) * A_CHUNK, hd * ahd:(hd + 1) * ahd]
            z_heads.append(jnp.dot(ws_ref[hd], vc, preferred_element_type=F32) + bsb_ref[hd])
        z_rows.append(jnp.concatenate(z_heads, axis=-1))
    y_a = u * jnp.concatenate(z_rows, axis=0)

    xg = jnp.dot(proj_scr[:, o_a:o_a + LANES], wal_ref[...], precision=HIGHEST,
                 preferred_element_type=F32) + bal_ref[...]
    la_scr[...] = (jnp.minimum(xg, 0.0) - jnp.log(1.0 + jnp.exp(-jnp.abs(xg)))) / B_GATE_TAU

    c = GLA_CHUNK
    row = lax.broadcasted_iota(I32, (c, c), 0)
    col = lax.broadcasted_iota(I32, (c, c), 1)
    causal = col <= row
    tri = causal.astype(F32)
    klane = lax.broadcasted_iota(I32, (1, kw), 1)
    bd_r = lax.broadcasted_iota(I32, (vw, kw), 0) // hv
    bd_c = lax.broadcasted_iota(I32, (vw, kw), 1) // hk
    bd_mask = (bd_r == bd_c).astype(F32)
    mid = c // 2 - 1

    def gla_chunk(j, carry):
        r0 = pl.multiple_of(j * c, c)
        g = la_scr[pl.ds(r0, c), :]
        q = proj_scr[pl.ds(r0, c), o_q:o_q + kw] * (hk ** -0.5)
        k = proj_scr[pl.ds(r0, c), o_k:o_k + kw]
        vv = proj_scr[pl.ds(r0, c), o_v:o_v + vw].astype(BF16)
        bcum = jnp.dot(tri, g, precision=HIGHEST, preferred_element_type=F32)
        blast = bcum[c - 1:c, :]
        bmid = bcum[mid:mid + 1, :]
        qe = (q * jnp.exp(bcum)).astype(BF16)
        qt = q * jnp.exp(bcum - bmid)
        kt = (k * jnp.exp(bmid - bcum)).astype(BF16)
        kd = (k * jnp.exp(blast - bcum)).astype(BF16)
        st = st_scr[...]
        o_inter = _nt(qe, st.astype(BF16))
        o_heads = []
        for hd in range(B_HEADS):
            m_h = jnp.logical_and(klane >= hd * hk, klane < (hd + 1) * hk)
            a_h = _nt(jnp.where(m_h, qt, 0.0).astype(BF16), kt)
            a_h = jnp.where(causal, a_h, 0.0).astype(BF16)
            o_heads.append(jnp.dot(a_h, vv[:, hd * hv:(hd + 1) * hv], preferred_element_type=F32))
        o_scr[pl.ds(r0, c), :] = o_inter + jnp.concatenate(o_heads, axis=-1)
        st_scr[...] = st * jnp.exp(blast) + _tn(vv, kd) * bd_mask
        return carry

    lax.fori_loop(0, ts // c, gla_chunk, 0)

    o = o_scr[...]
    og = og_ref[...]
    yb_heads = []
    for hd in range(B_HEADS):
        oh = o[:, hd * hv:(hd + 1) * hv]
        oh = oh * lax.rsqrt(jnp.mean(oh * oh, axis=-1, keepdims=True) + EPS)
        yb_heads.append(oh * og[:, hd * hv:(hd + 1) * hv])
    r = proj_scr[:, o_r:o_r + vw]
    y_b = jnp.concatenate(yb_heads, axis=-1) * (r * jax.nn.sigmoid(r))

    mixed = jnp.dot(jnp.concatenate([y_a, y_b], axis=-1).astype(BF16), wout_ref[...],
                    preferred_element_type=F32)
    xn = x + g1 * mixed
    xo_ref[...] = xn

    h2 = xn * lax.rsqrt(jnp.mean(xn * xn, axis=-1, keepdims=True) + EPS) * g2_ref[...]
    h2 = h2 * (1.0 + sc2) + sh2
    _rows_to_tiles(h2_ref, h2, ts)

    h2h = h2.astype(BF16)
    h2l = (h2 - h2h.astype(F32)).astype(BF16)
    wrh = wrh_ref[...]
    logits = (jnp.dot(h2h, wrh, preferred_element_type=F32)
              + jnp.dot(h2l, wrh, preferred_element_type=F32)
              + jnp.dot(h2h, wrl_ref[...], preferred_element_type=F32)) + br_ref[...]
    lane = lax.broadcasted_iota(I32, (ts, LANES), 1).astype(F32)
    neg = -jnp.inf
    big = float(LANES)
    is_g = lane < N_GROUPS
    gl = jnp.where(is_g, logits, neg)
    gmax = jnp.max(gl, axis=-1, keepdims=True)
    grp = jnp.min(jnp.where(gl == gmax, lane, big), axis=-1, keepdims=True)
    gsum = jnp.sum(jnp.where(is_g, jnp.exp(logits - gmax), 0.0), axis=-1, keepdims=True)
    p_grp = 1.0 / gsum
    lo = N_GROUPS + EXPERTS_PER_GROUP * grp
    el = jnp.where(jnp.logical_and(lane >= lo, lane < lo + EXPERTS_PER_GROUP), logits, neg)
    t1 = jnp.max(el, axis=-1, keepdims=True)
    i1 = jnp.min(jnp.where(el == t1, lane, big), axis=-1, keepdims=True)
    el2 = jnp.where(lane == i1, neg, el)
    t2 = jnp.max(el2, axis=-1, keepdims=True)
    i2 = jnp.min(jnp.where(el2 == t2, lane, big), axis=-1, keepdims=True)
    e2 = jnp.exp(t2 - t1)
    w0 = (1.0 / (1.0 + e2)) * p_grp
    w1 = (e2 / (1.0 + e2)) * p_grp
    e0 = i1 - N_GROUPS
    e1 = i2 - N_GROUPS

    sel0 = lane == e0
    sel1 = lane == e1
    onehot = jnp.logical_or(sel0, sel1)
    rr = lax.broadcasted_iota(I32, (ts, ts), 0)
    cc = lax.broadcasted_iota(I32, (ts, ts), 1)
    before = (cc < rr).astype(BF16)
    prefix = jnp.dot(before, onehot.astype(BF16), preferred_element_type=F32) + cnt_scr[0:1, :]
    rank0 = jnp.sum(jnp.where(sel0, prefix, 0.0), axis=-1, keepdims=True)
    rank1 = jnp.sum(jnp.where(sel1, prefix, 0.0), axis=-1, keepdims=True)
    cnt_new = cnt_scr[0:1, :] + jnp.sum(onehot.astype(F32), axis=0, keepdims=True)
    cnt_scr[...] = jnp.broadcast_to(cnt_new, cnt_scr.shape)
    cnt_ref[...] = jnp.broadcast_to(cnt_new, cnt_ref.shape)

    gate_ref[...] = jnp.where(lane == 0, w0, jnp.where(lane == 1, w1, 0.0))
    code_ref[...] = jnp.concatenate(
        [_col_to_lanes(e0 * CODE_SHIFT + rank0, ts), _col_to_lanes(e1 * CODE_SHIFT + rank1, ts)],
        axis=0).astype(I32)


def _mixer(l, x2, mod, g1, win, avg, ws, bsb, wal, bal, og, wout, g2, wrh, wrl, br, *, bsz, seq, ts):
    t, d = x2.shape
    n_s = seq // ts
    p_in = win.shape[-1]
    kw, vw = d // 4, d // 2
    lay2 = lambda b, s: (l, 0, 0)
    lay3 = lambda b, s: (l, 0, 0, 0)
    tok = lambda b, s: (b * n_s + s, 0)
    code_rows = 2 * ts // LANES
    kern = functools.partial(_mixer_kernel, ts=ts, d=d)
    return pl.pallas_call(
        kern,
        grid=(bsz, n_s),
        in_specs=[
            pl.BlockSpec((ts, d), tok),
            pl.BlockSpec((None, 1, 6, d), lambda b, s: (l, b, 0, 0)),
            pl.BlockSpec((None, 1, d), lay2),
            pl.BlockSpec((None, d, p_in), lay2),
            pl.BlockSpec((None, 1, d // 2), lay2),
            pl.BlockSpec((None, A_HEADS, A_CHUNK, A_CHUNK), lay3),
            pl.BlockSpec((None, A_HEADS, A_CHUNK, A_CHUNK), lay3),
            pl.BlockSpec((None, LANES, kw), lay2),
            pl.BlockSpec((None, 1, kw), lay2),
            pl.BlockSpec((None, 1, vw), lay2),
            pl.BlockSpec((None, d, d), lay2),
            pl.BlockSpec((None, 1, d), lay2),
            pl.BlockSpec((None, d, LANES), lay2),
            pl.BlockSpec((None, d, LANES), lay2),
            pl.BlockSpec((None, 1, LANES), lay2),
        ],
        out_specs=[
            pl.BlockSpec((ts, d), tok),
            pl.BlockSpec((ts * SUBLANES, LANES), tok),
            pl.BlockSpec((ts, LANES), tok),
            pl.BlockSpec((None, code_rows, LANES), lambda b, s: (b * n_s + s, 0, 0)),
            pl.BlockSpec((SUBLANES, LANES), lambda b, s: (0, 0)),
        ],
        out_shape=[
            jax.ShapeDtypeStruct((t, d), F32),
            jax.ShapeDtypeStruct((t * SUBLANES, LANES), F32),
            jax.ShapeDtypeStruct((t, LANES), F32),
            jax.ShapeDtypeStruct((t // ts, code_rows, LANES), I32),
            jax.ShapeDtypeStruct((SUBLANES, LANES), F32),
        ],
        scratch_shapes=[
            pltpu.VMEM((vw, kw), F32),
            pltpu.VMEM((ts, p_in), F32),
            pltpu.VMEM((ts, kw), F32),
            pltpu.VMEM((ts, vw), F32),
            pltpu.VMEM((SUBLANES, LANES), F32),
        ],
        compiler_params=pltpu.CompilerParams(
            dimension_semantics=("arbitrary", "arbitrary"), vmem_limit_bytes=VMEM_LIMIT),
        name="mixer",
    )(x2, mod, g1, win, avg, ws, bsb, wal, bal, og, wout, g2, wrh, wrl, br)


def _slots_kernel(code_ref, cnt_ref, d_ref, be_ref, bv_ref, *, blk):
    cnt = cnt_ref[0:1, :]
    padded = jnp.floor((cnt + (blk - 1.0)) / blk) * blk
    r = lax.broadcasted_iota(I32, (LANES, LANES), 0)
    c = lax.broadcasted_iota(I32, (LANES, LANES), 1)
    upper = (r <= c).astype(F32)
    pad_end = jnp.dot(jnp.broadcast_to(padded, (SUBLANES, LANES)), upper, precision=HIGHEST,
                      preferred_element_type=F32)[0:1, :]
    pad_start = pad_end - padded

    code = code_ref[...].astype(F32)
    e = jnp.floor(code / CODE_SHIFT)
    slot = code - e * CODE_SHIFT
    for k in range(N_EXPERTS):
        slot = slot + jnp.where(e == k, pad_start[:, k:k + 1], 0.0)
    d_ref[...] = slot.astype(I32)

    shp = be_ref.shape
    jpos = (lax.broadcasted_iota(I32, shp, 0) * LANES + lax.broadcasted_iota(I32, shp, 1)).astype(F32) * blk
    be = jnp.zeros(shp, F32)
    for k in range(N_EXPERTS):
        be = be + jnp.where(pad_end[:, k:k + 1] <= jpos, 1.0, 0.0)
    be = jnp.minimum(be, N_EXPERTS - 1.0)
    seg_end = jnp.zeros(shp, F32)
    for k in range(N_EXPERTS):
        seg_end = seg_end + jnp.where(be == k, pad_start[:, k:k + 1] + cnt[:, k:k + 1], 0.0)
    be_ref[...] = be.astype(I32)
    bv_ref[...] = jnp.clip(seg_end - jpos, 0.0, float(blk)).astype(I32)


def _slots(codes2, counts, *, blk):
    kern = functools.partial(_slots_kernel, blk=blk)
    return pl.pallas_call(
        kern,
        out_shape=[
            jax.ShapeDtypeStruct(codes2.shape, I32),
            jax.ShapeDtypeStruct((SUBLANES, LANES), I32),
            jax.ShapeDtypeStruct((SUBLANES, LANES), I32),
        ],
        compiler_params=pltpu.CompilerParams(vmem_limit_bytes=VMEM_LIMIT),
        name="slots",
    )(codes2, counts)


def _dispatch_kernel(d_ref, bv_ref, h2_ref, xpad_ref, zero_scr, sem0, sem1, semz,
                     *, td, unroll, blk, n_blocks, blocks_per_step):
    i = pl.program_id(0)
    base = i * (2 * td)

    @pl.when(i == 0)
    def _():
        zero_scr[...] = jnp.zeros_like(zero_scr)

    def row_copy(r, dst, sem):
        return pltpu.make_async_copy(
            h2_ref.at[pl.ds(pl.multiple_of(r * SUBLANES, SUBLANES), SUBLANES)],
            xpad_ref.at[pl.ds(pl.multiple_of(dst * SUBLANES, SUBLANES), SUBLANES)], sem)

    def issue(jj, carry):
        for uu in range(unroll):
            r = jj * unroll + uu
            row_copy(r, d_ref[base + r], sem0).start()
            row_copy(r, d_ref[base + td + r], sem1).start()
        return carry

    lax.fori_loop(0, td // unroll, issue, 0)

    def pad_fill(act):
        for jb in range(blocks_per_step):
            j = i * blocks_per_step + jb
            jc = jnp.minimum(j, n_blocks - 1)
            valid = bv_ref[jc]
            n_pad = jnp.where(j < n_blocks, blk - valid, 0)
            row = jc * blk + valid
            size = blk
            while size >= 1:
                hit = (n_pad & size) != 0

                @pl.when(hit)
                def _(row=row, size=size):
                    cp = pltpu.make_async_copy(
                        zero_scr.at[pl.ds(0, size * SUBLANES)],
                        xpad_ref.at[pl.ds(pl.multiple_of(row * SUBLANES, SUBLANES), size * SUBLANES)],
                        semz)
                    act(cp)

                row = row + jnp.where(hit, size, 0)
                size //= 2

    pad_fill(lambda cp: cp.start())

    def drain(jj, carry):
        for uu in range(unroll):
            r = jj * unroll + uu
            row_copy(r, d_ref[base + r], sem0).wait()
            row_copy(r, d_ref[base + td + r], sem1).wait()
        return carry

    lax.fori_loop(0, td // unroll, drain, 0)
    pad_fill(lambda cp: cp.wait())


def _dispatch(dflat, block_valid, h2t, *, n_blocks, blk, td, unroll=8):
    t = h2t.shape[0] // SUBLANES
    n_steps = t // td
    blocks_per_step = (n_blocks + n_steps - 1) // n_steps
    kern = functools.partial(_dispatch_kernel, td=td, unroll=unroll, blk=blk, n_blocks=n_blocks,
                             blocks_per_step=blocks_per_step)
    return pl.pallas_call(
        kern,
        grid_spec=pltpu.PrefetchScalarGridSpec(
            num_scalar_prefetch=2,
            grid=(n_steps,),
            in_specs=[pl.BlockSpec((td * SUBLANES, LANES), lambda i, dd, bv: (i, 0))],
            out_specs=pl.BlockSpec(memory_space=pl.ANY),
            scratch_shapes=[
                pltpu.VMEM((blk * SUBLANES, LANES), F32),
                pltpu.SemaphoreType.DMA, pltpu.SemaphoreType.DMA, pltpu.SemaphoreType.DMA],
        ),
        out_shape=jax.ShapeDtypeStruct((n_blocks * blk * SUBLANES, LANES), F32),
        compiler_params=pltpu.CompilerParams(
            dimension_semantics=("arbitrary",), vmem_limit_bytes=VMEM_LIMIT),
        name="dispatch",
    )(dflat, block_valid, h2t)


def _experts_kernel(be_ref, bv_ref, x_ref, w1_ref, w3_ref, w2_ref, y_ref, w1b, w3b, w2b, *, blk):
    j = pl.program_id(0)
    valid = bv_ref[j]
    used = valid > 0
    prev = be_ref[jnp.maximum(j - 1, 0)]
    fresh = jnp.logical_or(j == 0, be_ref[j] != prev)

    @pl.when(jnp.logical_and(used, fresh))
    def _():
        w1b[...] = w1_ref[...].astype(BF16)
        w3b[...] = w3_ref[...].astype(BF16)
        w2b[...] = w2_ref[...].astype(BF16)

    @pl.when(used)
    def _():
        x = _tiles_to_rows(x_ref, blk).astype(BF16)
        a = jnp.dot(x, w1b[...], preferred_element_type=F32)
        g = jnp.dot(x, w3b[...], preferred_element_type=F32)
        hmid = ((a * jax.nn.sigmoid(a)) * g).astype(BF16)
        y = jnp.dot(hmid, w2b[...], preferred_element_type=F32)
        _rows_to_tiles(y_ref, y, blk)

    @pl.when(jnp.logical_not(used))
    def _():
        y_ref[...] = jnp.zeros_like(y_ref)


def _experts(l, block_expert, block_valid, xpad, w1, w3, w2, *, n_blocks, blk):
    ne, d, de = w1.shape[1:]
    w1 = w1.reshape(-1, d, de)
    w3 = w3.reshape(-1, d, de)
    w2 = w2.reshape(-1, de, d)
    wmap = lambda j, be, bv: (l * ne + be[j], 0, 0)
    kern = functools.partial(_experts_kernel, blk=blk)
    return pl.pallas_call(
        kern,
        grid_spec=pltpu.PrefetchScalarGridSpec(
            num_scalar_prefetch=2,
            grid=(n_blocks,),
            in_specs=[
                pl.BlockSpec((blk * SUBLANES, LANES), lambda j, be, bv: (j, 0)),
                pl.BlockSpec((None, d, de), wmap),
                pl.BlockSpec((None, d, de), wmap),
                pl.BlockSpec((None, de, d), wmap),
            ],
            out_specs=pl.BlockSpec((blk * SUBLANES, LANES), lambda j, be, bv: (j, 0)),
            scratch_shapes=[
                pltpu.VMEM((d, de), BF16),
                pltpu.VMEM((d, de), BF16),
                pltpu.VMEM((de, d), BF16),
            ],
        ),
        out_shape=jax.ShapeDtypeStruct(xpad.shape, F32),
        compiler_params=pltpu.CompilerParams(
            dimension_semantics=("arbitrary",), vmem_limit_bytes=VMEM_LIMIT),
        name="experts",
    )(block_expert, block_valid, xpad, w1, w3, w2)


def _combine_kernel(d_ref, x_ref, gate_ref, mod_ref, fg_ref, ypad_ref, o_ref,
                    buf0, buf1, sem0, sem1, *, tc, unroll, final):
    i = pl.program_id(0)
    base = i * (2 * tc)

    def row_copy(r, src, buf, sem):
        return pltpu.make_async_copy(
            ypad_ref.at[pl.ds(pl.multiple_of(src * SUBLANES, SUBLANES), SUBLANES)],
            buf.at[pl.ds(pl.multiple_of(r * SUBLANES, SUBLANES), SUBLANES)], sem)

    def issue(jj, carry):
        for uu in range(unroll):
            r = jj * unroll + uu
            row_copy(r, d_ref[base + r], buf0, sem0).start()
            row_copy(r, d_ref[base + tc + r], buf1, sem1).start()
        return carry

    lax.fori_loop(0, tc // unroll, issue, 0)

    def drain(jj, carry):
        for uu in range(unroll):
            r = jj * unroll + uu
            row_copy(r, d_ref[base + r], buf0, sem0).wait()
            row_copy(r, d_ref[base + tc + r], buf1, sem1).wait()
        return carry

    lax.fori_loop(0, tc // unroll, drain, 0)

    gate = gate_ref[...]
    w0 = gate[:, 0:1]
    w1 = gate[:, 1:2]
    g2 = mod_ref[0][5:6]
    y0 = _tiles_to_rows(buf0, tc)
    y1 = _tiles_to_rows(buf1, tc)
    xo = x_ref[...] + g2 * (y0 * w0 + y1 * w1)
    if final:
        xo = xo * lax.rsqrt(jnp.mean(xo * xo, axis=-1, keepdims=True) + EPS) * fg_ref[...]
    o_ref[...] = xo


def _combine(l, dflat, x2, gate, mod, fgain, ypad, *, seq, final, tc, unroll=8):
    t, d = x2.shape
    per_b = seq // tc
    kern = functools.partial(_combine_kernel, tc=tc, unroll=unroll, final=final)
    return pl.pallas_call(
        kern,
        grid_spec=pltpu.PrefetchScalarGridSpec(
            num_scalar_prefetch=1,
            grid=(t // tc,),
            in_specs=[
                pl.BlockSpec((tc, d), lambda i, dd: (i, 0)),
                pl.BlockSpec((tc, LANES), lambda i, dd: (i, 0)),
                pl.BlockSpec((None, 1, 6, d), lambda i, dd: (l, i // per_b, 0, 0)),
                pl.BlockSpec((1, d), lambda i, dd: (0, 0)),
                pl.BlockSpec(memory_space=pl.ANY),
            ],
            out_specs=pl.BlockSpec((tc, d), lambda i, dd: (i, 0)),
            scratch_shapes=[
                pltpu.VMEM((tc * SUBLANES, LANES), F32),
                pltpu.VMEM((tc * SUBLANES, LANES), F32),
                pltpu.SemaphoreType.DMA,
                pltpu.SemaphoreType.DMA,
            ],
        ),
        out_shape=jax.ShapeDtypeStruct((t, d), F32),
        compiler_params=pltpu.CompilerParams(
            dimension_semantics=("arbitrary",), vmem_limit_bytes=VMEM_LIMIT),
        name="combine",
    )(dflat, x2, gate, mod, fgain, ypad)


@jax.jit
def _forward(x, c, w_ada, b_ada, norm1_gain, norm2_gain, w_in, a_v_gain, a_w_s, a_b_s, b_w_alpha,
             b_b_alpha, b_out_gain, w_out, w_group, b_group, w_router, b_router, w1, w3, w2,
             final_gain):
    bsz, seq, d = x.shape
    depth = w_ada.shape[0]
    t = bsz * seq
    ts = min(TOKEN_TILE, seq)
    blk = MOE_BLOCK
    n_blocks = (2 * t + blk - 1) // blk + N_EXPERTS
    assert t <= CODE_SHIFT and n_blocks <= SUBLANES * LANES and seq % ts == 0 and ts % LANES == 0
    p_in = w_in.shape[-1]
    p_main = p_in - B_GATE_RANK
    kw = d // 4

    mod = _ada(c, w_ada, b_ada).reshape(depth, bsz, 6, d)

    win = jnp.concatenate(
        [w_in[..., :p_main],
         jnp.pad(w_in[..., p_main:], ((0, 0), (0, 0), (0, LANES - B_GATE_RANK)))], axis=-1).astype(BF16)
    wal = jnp.pad(b_w_alpha, ((0, 0), (0, LANES - B_GATE_RANK), (0, 0)))
    causal = jnp.tril(jnp.ones((A_CHUNK, A_CHUNK), dtype=bool))
    ws = jnp.where(causal[None, None], a_w_s, 0.0).astype(BF16)
    bsb = jnp.broadcast_to(a_b_s[..., None], a_b_s.shape + (d // 2 // A_HEADS,))
    wout = w_out.astype(BF16)
    n_r = N_GROUPS + N_EXPERTS
    wr = jnp.pad(jnp.concatenate([w_group, w_router], axis=-1), ((0, 0), (0, 0), (0, LANES - n_r)))
    wrh = wr.astype(BF16)
    wrl = (wr - wrh.astype(F32)).astype(BF16)
    br = jnp.pad(jnp.concatenate([b_group, b_router], axis=-1), ((0, 0), (0, LANES - n_r)))

    g1 = norm1_gain.reshape(depth, 1, d)
    g2 = norm2_gain.reshape(depth, 1, d)
    avg = a_v_gain.reshape(depth, 1, d // 2)
    bal = b_b_alpha.reshape(depth, 1, kw)
    og = b_out_gain.reshape(depth, 1, d // 2)
    br = br.reshape(depth, 1, LANES)
    fg = final_gain.reshape(1, d)

    x2 = x.reshape(t, d)
    for l in range(depth):
        x2, h2t, gate, codes, counts = _mixer(
            l, x2, mod, g1, win, avg, ws, bsb, wal, bal, og, wout, g2, wrh, wrl, br,
            bsz=bsz, seq=seq, ts=ts)
        slots, block_expert, block_valid = _slots(codes.reshape(-1, LANES), counts, blk=blk)
        dflat = slots.reshape(-1)
        block_valid = block_valid.reshape(-1)
        xpad = _dispatch(dflat, block_valid, h2t, n_blocks=n_blocks, blk=blk, td=ts)
        ypad = _experts(l, block_expert.reshape(-1), block_valid, xpad, w1, w3, w2,
                        n_blocks=n_blocks, blk=blk)
        x2 = _combine(l, dflat, x2, gate, mod, fg, ypad, seq=seq, final=(l == depth - 1), tc=ts)
    return x2.reshape(bsz, seq, d)


def kernel(x, c, w_ada, b_ada, norm1_gain, norm2_gain, w_in, a_v_gain, a_w_s, a_b_s, b_w_alpha,
           b_b_alpha, b_out_gain, w_out, w_group, b_group, w_router, b_router, w1, w3, w2,
           final_gain):
    return _forward(x, c, w_ada, b_ada, norm1_gain, norm2_gain, w_in, a_v_gain, a_w_s, a_b_s,
                    b_w_alpha, b_b_alpha, b_out_gain, w_out, w_group, b_group, w_router, b_router,
                    w1, w3, w2, final_gain)
```

```python
import functools

import jax
import jax.numpy as jnp
from jax import lax
from jax.experimental import pallas as pl
from jax.experimental.pallas import tpu as pltpu

F32 = jnp.float32
BF16 = jnp.bfloat16
I32 = jnp.int32
HIGHEST = lax.Precision.HIGHEST

EPS = 1e-6
LANES = 128
SUBLANES = 8
A_HEADS = 4
A_CHUNK = 128
B_HEADS = 4
B_GATE_RANK = 16
B_GATE_TAU = 16.0
GLA_CHUNK = 128
N_GROUPS = 4
EXPERTS_PER_GROUP = 8
N_EXPERTS = N_GROUPS * EXPERTS_PER_GROUP
MOE_BLOCK = 256
TOKEN_TILE = 512
CODE_SHIFT = 65536.0
VMEM_LIMIT = 56 * 1024 * 1024


def _nt(a, b):
    return lax.dot_general(a, b, (((1,), (1,)), ((), ())), preferred_element_type=F32)


def _tn(a, b):
    return lax.dot_general(a, b, (((0,), (0,)), ((), ())), preferred_element_type=F32)


def _rows_to_tiles(ref, val, n_rows):
    for c in range(SUBLANES):
        ref[pl.ds(c, n_rows, stride=SUBLANES), :] = val[:, c * LANES:(c + 1) * LANES]


def _tiles_to_rows(ref, n_rows):
    return jnp.concatenate(
        [ref[pl.ds(c, n_rows, stride=SUBLANES), :] for c in range(SUBLANES)], axis=-1)


def _col_to_lanes(col, n):
    r = lax.broadcasted_iota(I32, (LANES, LANES), 0)
    c = lax.broadcasted_iota(I32, (LANES, LANES), 1)
    eye = r == c
    rows = [jnp.sum(jnp.where(eye, col[g * LANES:(g + 1) * LANES, :], 0.0), axis=0, keepdims=True)
            for g in range(n // LANES)]
    return jnp.concatenate(rows, axis=0)


def _ada_kernel(c_ref, w_ref, b_ref, o_ref):
    c = c_ref[...]
    c_act = c * jax.nn.sigmoid(c)
    o_ref[0] = jnp.dot(c_act, w_ref[0], precision=HIGHEST, preferred_element_type=F32) + b_ref[0]


def _ada(c, w_ada, b_ada, tn=1024):
    depth, d, n = w_ada.shape
    bsz = c.shape[0]
    return pl.pallas_call(
        _ada_kernel,
        grid=(depth, n // tn),
        in_specs=[
            pl.BlockSpec((bsz, d), lambda l, j: (0, 0)),
            pl.BlockSpec((1, d, tn), lambda l, j: (l, 0, j)),
            pl.BlockSpec((1, 1, tn), lambda l, j: (l, 0, j)),
        ],
        out_specs=pl.BlockSpec((1, bsz, tn), lambda l, j: (l, 0, j)),
        out_shape=jax.ShapeDtypeStruct((depth, bsz, n), F32),
        compiler_params=pltpu.CompilerParams(
            dimension_semantics=("arbitrary", "arbitrary"), vmem_limit_bytes=VMEM_LIMIT),
        name="ada_mod",
    )(c, w_ada, b_ada.reshape(depth, 1, n))


def _gather_rows_start(d_ref, base, ypad_ref, buf0, buf1, sem0, sem1, *, n, unroll):
    def row_copy(r, src, buf, sem):
        return pltpu.make_async_copy(
            ypad_ref.at[pl.ds(pl.multiple_of(src * SUBLANES, SUBLANES), SUBLANES)],
            buf.at[pl.ds(pl.multiple_of(r * SUBLANES, SUBLANES), SUBLANES)], sem)

    def issue(jj, carry):
        for uu in range(unroll):
            r = jj * unroll + uu
            row_copy(r, d_ref[base + r], buf0, sem0).start()
            row_copy(r, d_ref[base + n + r], buf1, sem1).start()
        return carry

    lax.fori_loop(0, n // unroll, issue, 0)


def _gather_rows_wait(ypad_ref, buf0, buf1, sem0, sem1, *, n):
    pltpu.make_async_copy(ypad_ref.at[pl.ds(0, n * SUBLANES)], buf0, sem0).wait()
    pltpu.make_async_copy(ypad_ref.at[pl.ds(0, n * SUBLANES)], buf1, sem1).wait()


def _gathered_rows_pipeline(i, n_tiles, d_ref, ypad_ref, y0buf, y1buf, sems, *, n, unroll, consume):
    start = functools.partial(_gather_rows_start, d_ref, n=n, unroll=unroll)

    @pl.when(i == 0)
    def _():
        start(0, ypad_ref, y0buf.at[0], y1buf.at[0], sems.at[0, 0], sems.at[0, 1])

    for p in range(2):
        @pl.when(i % 2 == p)
        def _(p=p):
            _gather_rows_wait(ypad_ref, y0buf.at[p], y1buf.at[p], sems.at[p, 0], sems.at[p, 1], n=n)

            @pl.when(i + 1 < n_tiles)
            def _():
                q = 1 - p
                start((i + 1) * (2 * n), ypad_ref, y0buf.at[q], y1buf.at[q],
                      sems.at[q, 0], sems.at[q, 1])

            consume(_tiles_to_rows(y0buf.at[p], n), _tiles_to_rows(y1buf.at[p], n))


def _mixer_kernel(*refs, ts, d, fused, unroll):
    if fused:
        d_ref, x_ref, gatep_ref, modp_ref, ypad_ref = refs[:5]
        refs = refs[5:]
        xin_scr, y0buf, y1buf, sems = refs[-4:]
        refs = refs[:-4]
    else:
        x_ref = refs[0]
        refs = refs[1:]
    (mod_ref, g1_ref, win_ref, avg_ref, ws_ref, bsb_ref, wal_ref, bal_ref, og_ref, wout_ref, g2_ref,
     wrh_ref, wrl_ref, br_ref, xo_ref, h2_ref, gate_ref, code_ref, cnt_ref,
     st_scr, proj_scr, la_scr, o_scr, cnt_scr) = refs
    a_w = d // 2
    kw = d // 4
    vw = d // 2
    hk = kw // B_HEADS
    hv = vw // B_HEADS
    o_q, o_k, o_v, o_r, o_a = 2 * a_w, 2 * a_w + kw, 2 * a_w + 2 * kw, 2 * a_w + 2 * kw + vw, \
        2 * a_w + 2 * kw + 2 * vw

    b = pl.program_id(0)
    s = pl.program_id(1)

    @pl.when(s == 0)
    def _():
        st_scr[...] = jnp.zeros_like(st_scr)

    @pl.when(jnp.logical_and(b == 0, s == 0))
    def _():
        cnt_scr[...] = jnp.zeros_like(cnt_scr)

    if fused:
        def consume(y0, y1):
            gate_p = gatep_ref[...]
            moe = y0 * gate_p[:, 0:1] + y1 * gate_p[:, 1:2]
            xin_scr[...] = x_ref[...] + modp_ref[0][5:6] * moe

        _gathered_rows_pipeline(b * pl.num_programs(1) + s, pl.num_programs(0) * pl.num_programs(1),
                                d_ref, ypad_ref, y0buf, y1buf, sems, n=ts, unroll=unroll,
                                consume=consume)
        x = xin_scr[...]
    else:
        x = x_ref[...]
    mod = mod_ref[0]
    sh1, sc1, g1 = mod[0:1], mod[1:2], mod[2:3]
    sh2, sc2 = mod[3:4], mod[4:5]

    h = x * lax.rsqrt(jnp.mean(x * x, axis=-1, keepdims=True) + EPS) * g1_ref[...]
    h = h * (1.0 + sc1) + sh1
    proj_scr[...] = jnp.dot(h.astype(BF16), win_ref[...], preferred_element_type=F32)

    u = jax.nn.gelu(proj_scr[:, 0:a_w])
    v = jax.nn.gelu(proj_scr[:, a_w:2 * a_w])
    vn = v * lax.rsqrt(jnp.mean(v * v, axis=-1, keepdims=True) + EPS) * avg_ref[...]
    vn = vn.astype(BF16)
    ahd = a_w // A_HEADS
    z_rows = []
    for ci in range(ts // A_CHUNK):
        z_heads = []
        for hd in range(A_HEADS):
            vc = vn[ci * A_CHUNK:(ci + 1) * A_CHUNK, hd * ahd:(hd + 1) * ahd]
            z_heads.append(jnp.dot(ws_ref[hd], vc, preferred_element_type=F32) + bsb_ref[hd])
        z_rows.append(jnp.concatenate(z_heads, axis=-1))
    y_a = u * jnp.concatenate(z_rows, axis=0)

    xg = jnp.dot(proj_scr[:, o_a:o_a + LANES], wal_ref[...], precision=HIGHEST,
                 preferred_element_type=F32) + bal_ref[...]
    la_scr[...] = (jnp.minimum(xg, 0.0) - jnp.log(1.0 + jnp.exp(-jnp.abs(xg)))) / B_GATE_TAU

    c = GLA_CHUNK
    row = lax.broadcasted_iota(I32, (c, c), 0)
    col = lax.broadcasted_iota(I32, (c, c), 1)
    causal = col <= row
    tri = causal.astype(F32)
    klane = lax.broadcasted_iota(I32, (1, kw), 1)
    bd_r = lax.broadcasted_iota(I32, (vw, kw), 0) // hv
    bd_c = lax.broadcasted_iota(I32, (vw, kw), 1) // hk
    bd_mask = (bd_r == bd_c).astype(F32)
    mid = c // 2 - 1

    def gla_chunk(j, carry):
        r0 = pl.multiple_of(j * c, c)
        g = la_scr[pl.ds(r0, c), :]
        q = proj_scr[pl.ds(r0, c), o_q:o_q + kw] * (hk ** -0.5)
        k = proj_scr[pl.ds(r0, c), o_k:o_k + kw]
        vv = proj_scr[pl.ds(r0, c), o_v:o_v + vw].astype(BF16)
        bcum = jnp.dot(tri, g, precision=HIGHEST, preferred_element_type=F32)
        blast = bcum[c - 1:c, :]
        bmid = bcum[mid:mid + 1, :]
        qe = (q * jnp.exp(bcum)).astype(BF16)
        qt = q * jnp.exp(bcum - bmid)
        kt = (k * jnp.exp(bmid - bcum)).astype(BF16)
        kd = (k * jnp.exp(blast - bcum)).astype(BF16)
        st = st_scr[...]
        o_inter = _nt(qe, st.astype(BF16))
        o_heads = []
        for hd in range(B_HEADS):
            m_h = jnp.logical_and(klane >= hd * hk, klane < (hd + 1) * hk)
            a_h = _nt(jnp.where(m_h, qt, 0.0).astype(BF16), kt)
            a_h = jnp.where(causal, a_h, 0.0).astype(BF16)
            o_heads.append(jnp.dot(a_h, vv[:, hd * hv:(hd + 1) * hv], preferred_element_type=F32))
        o_scr[pl.ds(r0, c), :] = o_inter + jnp.concatenate(o_heads, axis=-1)
        st_scr[...] = st * jnp.exp(blast) + _tn(vv, kd) * bd_mask
        return carry

    lax.fori_loop(0, ts // c, gla_chunk, 0)

    o = o_scr[...]
    og = og_ref[...]
    yb_heads = []
    for hd in range(B_HEADS):
        oh = o[:, hd * hv:(hd + 1) * hv]
        oh = oh * lax.rsqrt(jnp.mean(oh * oh, axis=-1, keepdims=True) + EPS)
        yb_heads.append(oh * og[:, hd * hv:(hd + 1) * hv])
    r = proj_scr[:, o_r:o_r + vw]
    y_b = jnp.concatenate(yb_heads, axis=-1) * (r * jax.nn.sigmoid(r))

    mixed = jnp.dot(jnp.concatenate([y_a, y_b], axis=-1).astype(BF16), wout_ref[...],
                    preferred_element_type=F32)
    xn = x + g1 * mixed
    xo_ref[...] = xn

    h2 = xn * lax.rsqrt(jnp.mean(xn * xn, axis=-1, keepdims=True) + EPS) * g2_ref[...]
    h2 = h2 * (1.0 + sc2) + sh2
    _rows_to_tiles(h2_ref, h2, ts)

    h2h = h2.astype(BF16)
    h2l = (h2 - h2h.astype(F32)).astype(BF16)
    wrh = wrh_ref[...]
    logits = (jnp.dot(h2h, wrh, preferred_element_type=F32)
              + jnp.dot(h2l, wrh, preferred_element_type=F32)
              + jnp.dot(h2h, wrl_ref[...], preferred_element_type=F32)) + br_ref[...]
    lane = lax.broadcasted_iota(I32, (ts, LANES), 1).astype(F32)
    neg = -jnp.inf
    big = float(LANES)
    is_g = lane < N_GROUPS
    gl = jnp.where(is_g, logits, neg)
    gmax = jnp.max(gl, axis=-1, keepdims=True)
    grp = jnp.min(jnp.where(gl == gmax, lane, big), axis=-1, keepdims=True)
    gsum = jnp.sum(jnp.where(is_g, jnp.exp(logits - gmax), 0.0), axis=-1, keepdims=True)
    p_grp = 1.0 / gsum
    lo = N_GROUPS + EXPERTS_PER_GROUP * grp
    el = jnp.where(jnp.logical_and(lane >= lo, lane < lo + EXPERTS_PER_GROUP), logits, neg)
    t1 = jnp.max(el, axis=-1, keepdims=True)
    i1 = jnp.min(jnp.where(el == t1, lane, big), axis=-1, keepdims=True)
    el2 = jnp.where(lane == i1, neg, el)
    t2 = jnp.max(el2, axis=-1, keepdims=True)
    i2 = jnp.min(jnp.where(el2 == t2, lane, big), axis=-1, keepdims=True)
    e2 = jnp.exp(t2 - t1)
    w0 = (1.0 / (1.0 + e2)) * p_grp
    w1 = (e2 / (1.0 + e2)) * p_grp
    e0 = i1 - N_GROUPS
    e1 = i2 - N_GROUPS

    sel0 = lane == e0
    sel1 = lane == e1
    onehot = jnp.logical_or(sel0, sel1)
    rr = lax.broadcasted_iota(I32, (ts, ts), 0)
    cc = lax.broadcasted_iota(I32, (ts, ts), 1)
    before = (cc < rr).astype(BF16)
    prefix = jnp.dot(before, onehot.astype(BF16), preferred_element_type=F32) + cnt_scr[0:1, :]
    rank0 = jnp.sum(jnp.where(sel0, prefix, 0.0), axis=-1, keepdims=True)
    rank1 = jnp.sum(jnp.where(sel1, prefix, 0.0), axis=-1, keepdims=True)
    cnt_new = cnt_scr[0:1, :] + jnp.sum(onehot.astype(F32), axis=0, keepdims=True)
    cnt_scr[...] = jnp.broadcast_to(cnt_new, cnt_scr.shape)
    cnt_ref[...] = jnp.broadcast_to(cnt_new, cnt_ref.shape)

    gate_ref[...] = jnp.where(lane == 0, w0, jnp.where(lane == 1, w1, 0.0))
    code_ref[...] = jnp.concatenate(
        [_col_to_lanes(e0 * CODE_SHIFT + rank0, ts), _col_to_lanes(e1 * CODE_SHIFT + rank1, ts)],
        axis=0).astype(I32)


def _mixer(l, x2, mod, g1, win, avg, ws, bsb, wal, bal, og, wout, g2, wrh, wrl, br, *, bsz, seq, ts,
           moe_prev=None, unroll=8):
    t, d = x2.shape
    n_s = seq // ts
    p_in = win.shape[-1]
    kw, vw = d // 4, d // 2
    lay2 = lambda b, s, *_: (l, 0, 0)
    lay3 = lambda b, s, *_: (l, 0, 0, 0)
    tok = lambda b, s, *_: (b * n_s + s, 0)
    code_rows = 2 * ts // LANES
    fused = moe_prev is not None
    kern = functools.partial(_mixer_kernel, ts=ts, d=d, fused=fused, unroll=unroll)
    x_specs = [pl.BlockSpec((ts, d), tok)]
    x_args = [x2]
    extra_scratch = []
    prefetch = []
    if fused:
        dflat, gate_prev, ypad_prev = moe_prev
        prefetch = [dflat]
        x_specs += [
            pl.BlockSpec((ts, LANES), tok),
            pl.BlockSpec((None, 1, 6, d), lambda b, s, *_: (l - 1, b, 0, 0)),
            pl.BlockSpec(memory_space=pl.ANY),
        ]
        x_args += [gate_prev, mod, ypad_prev]
        extra_scratch = [
            pltpu.VMEM((ts, d), F32),
            pltpu.VMEM((2, ts * SUBLANES, LANES), F32),
            pltpu.VMEM((2, ts * SUBLANES, LANES), F32),
            pltpu.SemaphoreType.DMA((2, 2)),
        ]
    grid_spec = pltpu.PrefetchScalarGridSpec(
        num_scalar_prefetch=len(prefetch),
        grid=(bsz, n_s),
        in_specs=x_specs + [
            pl.BlockSpec((None, 1, 6, d), lambda b, s, *_: (l, b, 0, 0)),
            pl.BlockSpec((None, 1, d), lay2),
            pl.BlockSpec((None, d, p_in), lay2),
            pl.BlockSpec((None, 1, d // 2), lay2),
            pl.BlockSpec((None, A_HEADS, A_CHUNK, A_CHUNK), lay3),
            pl.BlockSpec((None, A_HEADS, A_CHUNK, A_CHUNK), lay3),
            pl.BlockSpec((None, LANES, kw), lay2),
            pl.BlockSpec((None, 1, kw), lay2),
            pl.BlockSpec((None, 1, vw), lay2),
            pl.BlockSpec((None, d, d), lay2),
            pl.BlockSpec((None, 1, d), lay2),
            pl.BlockSpec((None, d, LANES), lay2),
            pl.BlockSpec((None, d, LANES), lay2),
            pl.BlockSpec((None, 1, LANES), lay2),
        ],
        out_specs=[
            pl.BlockSpec((ts, d), tok),
            pl.BlockSpec((ts * SUBLANES, LANES), tok),
            pl.BlockSpec((ts, LANES), tok),
            pl.BlockSpec((None, code_rows, LANES), lambda b, s, *_: (b * n_s + s, 0, 0)),
            pl.BlockSpec((SUBLANES, LANES), lambda b, s, *_: (0, 0)),
        ],
        scratch_shapes=[
            pltpu.VMEM((vw, kw), F32),
            pltpu.VMEM((ts, p_in), F32),
            pltpu.VMEM((ts, kw), F32),
            pltpu.VMEM((ts, vw), F32),
            pltpu.VMEM((SUBLANES, LANES), F32),
        ] + extra_scratch,
    )
    return pl.pallas_call(
        kern,
        grid_spec=grid_spec,
        out_shape=[
            jax.ShapeDtypeStruct((t, d), F32),
            jax.ShapeDtypeStruct((t * SUBLANES, LANES), F32),
            jax.ShapeDtypeStruct((t, LANES), F32),
            jax.ShapeDtypeStruct((t // ts, code_rows, LANES), I32),
            jax.ShapeDtypeStruct((SUBLANES, LANES), F32),
        ],
        compiler_params=pltpu.CompilerParams(
            dimension_semantics=("arbitrary", "arbitrary"), vmem_limit_bytes=VMEM_LIMIT),
        name="mixer",
    )(*prefetch, *x_args, mod, g1, win, avg, ws, bsb, wal, bal, og, wout, g2, wrh, wrl, br)


def _slots_kernel(code_ref, cnt_ref, d_ref, be_ref, bv_ref, *, blk):
    cnt = cnt_ref[0:1, :]
    padded = jnp.floor((cnt + (blk - 1.0)) / blk) * blk
    r = lax.broadcasted_iota(I32, (LANES, LANES), 0)
    c = lax.broadcasted_iota(I32, (LANES, LANES), 1)
    upper = (r <= c).astype(F32)
    pad_end = jnp.dot(jnp.broadcast_to(padded, (SUBLANES, LANES)), upper, precision=HIGHEST,
                      preferred_element_type=F32)[0:1, :]
    pad_start = pad_end - padded

    code = code_ref[...].astype(F32)
    e = jnp.floor(code / CODE_SHIFT)
    slot = code - e * CODE_SHIFT
    for k in range(N_EXPERTS):
        slot = slot + jnp.where(e == k, pad_start[:, k:k + 1], 0.0)
    d_ref[...] = slot.astype(I32)

    shp = be_ref.shape
    jpos = (lax.broadcasted_iota(I32, shp, 0) * LANES + lax.broadcasted_iota(I32, shp, 1)).astype(F32) * blk
    be = jnp.zeros(shp, F32)
    for k in range(N_EXPERTS):
        be = be + jnp.where(pad_end[:, k:k + 1] <= jpos, 1.0, 0.0)
    be = jnp.minimum(be, N_EXPERTS - 1.0)
    seg_end = jnp.zeros(shp, F32)
    for k in range(N_EXPERTS):
        seg_end = seg_end + jnp.where(be == k, pad_start[:, k:k + 1] + cnt[:, k:k + 1], 0.0)
    be_ref[...] = be.astype(I32)
    bv_ref[...] = jnp.clip(seg_end - jpos, 0.0, float(blk)).astype(I32)


def _slots(codes2, counts, *, blk):
    kern = functools.partial(_slots_kernel, blk=blk)
    return pl.pallas_call(
        kern,
        out_shape=[
            jax.ShapeDtypeStruct(codes2.shape, I32),
            jax.ShapeDtypeStruct((SUBLANES, LANES), I32),
            jax.ShapeDtypeStruct((SUBLANES, LANES), I32),
        ],
        compiler_params=pltpu.CompilerParams(vmem_limit_bytes=VMEM_LIMIT),
        name="slots",
    )(codes2, counts)


def _dispatch_kernel(d_ref, bv_ref, h2_ref, xpad_ref, zero_scr, sem0, sem1, semz,
                     *, td, unroll, blk, n_blocks, blocks_per_step):
    i = pl.program_id(0)
    base = i * (2 * td)

    @pl.when(i == 0)
    def _():
        zero_scr[...] = jnp.zeros_like(zero_scr)

    def row_copy(r, dst, sem):
        return pltpu.make_async_copy(
            h2_ref.at[pl.ds(pl.multiple_of(r * SUBLANES, SUBLANES), SUBLANES)],
            xpad_ref.at[pl.ds(pl.multiple_of(dst * SUBLANES, SUBLANES), SUBLANES)], sem)

    def issue(jj, carry):
        for uu in range(unroll):
            r = jj * unroll + uu
            row_copy(r, d_ref[base + r], sem0).start()
            row_copy(r, d_ref[base + td + r], sem1).start()
        return carry

    lax.fori_loop(0, td // unroll, issue, 0)

    def pad_fill(act):
        for jb in range(blocks_per_step):
            j = i * blocks_per_step + jb
            jc = jnp.minimum(j, n_blocks - 1)
            valid = bv_ref[jc]
            n_pad = jnp.where(j < n_blocks, blk - valid, 0)
            row = jc * blk + valid
            size = blk
            while size >= 1:
                hit = (n_pad & size) != 0

                @pl.when(hit)
                def _(row=row, size=size):
                    cp = pltpu.make_async_copy(
                        zero_scr.at[pl.ds(0, size * SUBLANES)],
                        xpad_ref.at[pl.ds(pl.multiple_of(row * SUBLANES, SUBLANES), size * SUBLANES)],
                        semz)
                    act(cp)

                row = row + jnp.where(hit, size, 0)
                size //= 2

    pad_fill(lambda cp: cp.start())

    def drain(jj, carry):
        for uu in range(unroll):
            r = jj * unroll + uu
            row_copy(r, d_ref[base + r], sem0).wait()
            row_copy(r, d_ref[base + td + r], sem1).wait()
        return carry

    lax.fori_loop(0, td // unroll, drain, 0)
    pad_fill(lambda cp: cp.wait())


def _dispatch(dflat, block_valid, h2t, *, n_blocks, blk, td, unroll=8):
    t = h2t.shape[0] // SUBLANES
    n_steps = t // td
    blocks_per_step = (n_blocks + n_steps - 1) // n_steps
    kern = functools.partial(_dispatch_kernel, td=td, unroll=unroll, blk=blk, n_blocks=n_blocks,
                             blocks_per_step=blocks_per_step)
    return pl.pallas_call(
        kern,
        grid_spec=pltpu.PrefetchScalarGridSpec(
            num_scalar_prefetch=2,
            grid=(n_steps,),
            in_specs=[pl.BlockSpec((td * SUBLANES, LANES), lambda i, dd, bv: (i, 0))],
            out_specs=pl.BlockSpec(memory_space=pl.ANY),
            scratch_shapes=[
                pltpu.VMEM((blk * SUBLANES, LANES), F32),
                pltpu.SemaphoreType.DMA, pltpu.SemaphoreType.DMA, pltpu.SemaphoreType.DMA],
        ),
        out_shape=jax.ShapeDtypeStruct((n_blocks * blk * SUBLANES, LANES), F32),
        compiler_params=pltpu.CompilerParams(
            dimension_semantics=("arbitrary",), vmem_limit_bytes=VMEM_LIMIT),
        name="dispatch",
    )(dflat, block_valid, h2t)


def _experts_kernel(be_ref, bv_ref, x_ref, w1_ref, w3_ref, w2_ref, y_ref, w1b, w3b, w2b, *, blk):
    j = pl.program_id(0)
    valid = bv_ref[j]
    used = valid > 0
    prev = be_ref[jnp.maximum(j - 1, 0)]
    fresh = jnp.logical_or(j == 0, be_ref[j] != prev)

    @pl.when(jnp.logical_and(used, fresh))
    def _():
        w1b[...] = w1_ref[...].astype(BF16)
        w3b[...] = w3_ref[...].astype(BF16)
        w2b[...] = w2_ref[...].astype(BF16)

    @pl.when(used)
    def _():
        x = _tiles_to_rows(x_ref, blk).astype(BF16)
        a = jnp.dot(x, w1b[...], preferred_element_type=F32)
        g = jnp.dot(x, w3b[...], preferred_element_type=F32)
        hmid = ((a * jax.nn.sigmoid(a)) * g).astype(BF16)
        y = jnp.dot(hmid, w2b[...], preferred_element_type=F32)
        _rows_to_tiles(y_ref, y, blk)

    @pl.when(jnp.logical_not(used))
    def _():
        y_ref[...] = jnp.zeros_like(y_ref)


def _experts(l, block_expert, block_valid, xpad, w1, w3, w2, *, n_blocks, blk):
    ne, d, de = w1.shape[1:]
    w1 = w1.reshape(-1, d, de)
    w3 = w3.reshape(-1, d, de)
    w2 = w2.reshape(-1, de, d)
    wmap = lambda j, be, bv: (l * ne + be[j], 0, 0)
    kern = functools.partial(_experts_kernel, blk=blk)
    return pl.pallas_call(
        kern,
        grid_spec=pltpu.PrefetchScalarGridSpec(
            num_scalar_prefetch=2,
            grid=(n_blocks,),
            in_specs=[
                pl.BlockSpec((blk * SUBLANES, LANES), lambda j, be, bv: (j, 0)),
                pl.BlockSpec((None, d, de), wmap),
                pl.BlockSpec((None, d, de), wmap),
                pl.BlockSpec((None, de, d), wmap),
            ],
            out_specs=pl.BlockSpec((blk * SUBLANES, LANES), lambda j, be, bv: (j, 0)),
            scratch_shapes=[
                pltpu.VMEM((d, de), BF16),
                pltpu.VMEM((d, de), BF16),
                pltpu.VMEM((de, d), BF16),
            ],
        ),
        out_shape=jax.ShapeDtypeStruct(xpad.shape, F32),
        compiler_params=pltpu.CompilerParams(
            dimension_semantics=("arbitrary",), vmem_limit_bytes=VMEM_LIMIT),
        name="experts",
    )(block_expert, block_valid, xpad, w1, w3, w2)


def _combine_kernel(d_ref, x_ref, gate_ref, mod_ref, fg_ref, ypad_ref, o_ref,
                    y0buf, y1buf, sems, *, tc, unroll):
    def consume(y0, y1):
        gate = gate_ref[...]
        xo = x_ref[...] + mod_ref[0][5:6] * (y0 * gate[:, 0:1] + y1 * gate[:, 1:2])
        o_ref[...] = xo * lax.rsqrt(jnp.mean(xo * xo, axis=-1, keepdims=True) + EPS) * fg_ref[...]

    _gathered_rows_pipeline(pl.program_id(0), pl.num_programs(0), d_ref, ypad_ref, y0buf, y1buf, sems,
                            n=tc, unroll=unroll, consume=consume)


def _combine(l, dflat, x2, gate, mod, fgain, ypad, *, seq, tc, unroll=8):
    t, d = x2.shape
    per_b = seq // tc
    kern = functools.partial(_combine_kernel, tc=tc, unroll=unroll)
    return pl.pallas_call(
        kern,
        grid_spec=pltpu.PrefetchScalarGridSpec(
            num_scalar_prefetch=1,
            grid=(t // tc,),
            in_specs=[
                pl.BlockSpec((tc, d), lambda i, dd: (i, 0)),
                pl.BlockSpec((tc, LANES), lambda i, dd: (i, 0)),
                pl.BlockSpec((None, 1, 6, d), lambda i, dd: (l, i // per_b, 0, 0)),
                pl.BlockSpec((1, d), lambda i, dd: (0, 0)),
                pl.BlockSpec(memory_space=pl.ANY),
            ],
            out_specs=pl.BlockSpec((tc, d), lambda i, dd: (i, 0)),
            scratch_shapes=[
                pltpu.VMEM((2, tc * SUBLANES, LANES), F32),
                pltpu.VMEM((2, tc * SUBLANES, LANES), F32),
                pltpu.SemaphoreType.DMA((2, 2)),
            ],
        ),
        out_shape=jax.ShapeDtypeStruct((t, d), F32),
        compiler_params=pltpu.CompilerParams(
            dimension_semantics=("arbitrary",), vmem_limit_bytes=VMEM_LIMIT),
        name="combine",
    )(dflat, x2, gate, mod, fgain, ypad)


@jax.jit
def _forward(x, c, w_ada, b_ada, norm1_gain, norm2_gain, w_in, a_v_gain, a_w_s, a_b_s, b_w_alpha,
             b_b_alpha, b_out_gain, w_out, w_group, b_group, w_router, b_router, w1, w3, w2,
             final_gain):
    bsz, seq, d = x.shape
    depth = w_ada.shape[0]
    t = bsz * seq
    ts = min(TOKEN_TILE, seq)
    blk = MOE_BLOCK
    n_blocks = (2 * t + blk - 1) // blk + N_EXPERTS
    assert t <= CODE_SHIFT and n_blocks <= SUBLANES * LANES and seq % ts == 0 and ts % LANES == 0
    p_in = w_in.shape[-1]
    p_main = p_in - B_GATE_RANK
    kw = d // 4

    mod = _ada(c, w_ada, b_ada).reshape(depth, bsz, 6, d)

    win = jnp.concatenate(
        [w_in[..., :p_main],
         jnp.pad(w_in[..., p_main:], ((0, 0), (0, 0), (0, LANES - B_GATE_RANK)))], axis=-1).astype(BF16)
    wal = jnp.pad(b_w_alpha, ((0, 0), (0, LANES - B_GATE_RANK), (0, 0)))
    causal = jnp.tril(jnp.ones((A_CHUNK, A_CHUNK), dtype=bool))
    ws = jnp.where(causal[None, None], a_w_s, 0.0).astype(BF16)
    bsb = jnp.broadcast_to(a_b_s[..., None], a_b_s.shape + (d // 2 // A_HEADS,))
    wout = w_out.astype(BF16)
    n_r = N_GROUPS + N_EXPERTS
    wr = jnp.pad(jnp.concatenate([w_group, w_router], axis=-1), ((0, 0), (0, 0), (0, LANES - n_r)))
    wrh = wr.astype(BF16)
    wrl = (wr - wrh.astype(F32)).astype(BF16)
    br = jnp.pad(jnp.concatenate([b_group, b_router], axis=-1), ((0, 0), (0, LANES - n_r)))

    g1 = norm1_gain.reshape(depth, 1, d)
    g2 = norm2_gain.reshape(depth, 1, d)
    avg = a_v_gain.reshape(depth, 1, d // 2)
    bal = b_b_alpha.reshape(depth, 1, kw)
    og = b_out_gain.reshape(depth, 1, d // 2)
    br = br.reshape(depth, 1, LANES)
    fg = final_gain.reshape(1, d)

    x2 = x.reshape(t, d)
    moe_prev = None
    for l in range(depth):
        x2, h2t, gate, codes, counts = _mixer(
            l, x2, mod, g1, win, avg, ws, bsb, wal, bal, og, wout, g2, wrh, wrl, br,
            bsz=bsz, seq=seq, ts=ts, moe_prev=moe_prev)
        slots, block_expert, block_valid = _slots(codes.reshape(-1, LANES), counts, blk=blk)
        dflat = slots.reshape(-1)
        block_valid = block_valid.reshape(-1)
        xpad = _dispatch(dflat, block_valid, h2t, n_blocks=n_blocks, blk=blk, td=ts)
        ypad = _experts(l, block_expert.reshape(-1), block_valid, xpad, w1, w3, w2,
                        n_blocks=n_blocks, blk=blk)
        moe_prev = (dflat, gate, ypad)
    x2 = _combine(depth - 1, dflat, x2, gate, mod, fg, ypad, seq=seq, tc=ts)
    return x2.reshape(bsz, seq, d)


def kernel(x, c, w_ada, b_ada, norm1_gain, norm2_gain, w_in, a_v_gain, a_w_s, a_b_s, b_w_alpha,
           b_b_alpha, b_out_gain, w_out, w_group, b_group, w_router, b_router, w1, w3, w2,
           final_gain):
    return _forward(x, c, w_ada, b_ada, norm1_gain, norm2_gain, w_in, a_v_gain, a_w_s, a_b_s,
                    b_w_alpha, b_b_alpha, b_out_gain, w_out, w_group, b_group, w_router, b_router,
                    w1, w3, w2, final_gain)
```

```python
import functools

import jax
import jax.numpy as jnp
from jax import lax
from jax.experimental import pallas as pl
from jax.experimental.pallas import tpu as pltpu

F32 = jnp.float32
BF16 = jnp.bfloat16
I32 = jnp.int32
HIGHEST = lax.Precision.HIGHEST

EPS = 1e-6
LANES = 128
SUBLANES = 8
A_HEADS = 4
A_CHUNK = 128
B_HEADS = 4
B_GATE_RANK = 16
B_GATE_TAU = 16.0
GLA_CHUNK = 128
N_GROUPS = 4
EXPERTS_PER_GROUP = 8
N_EXPERTS = N_GROUPS * EXPERTS_PER_GROUP
MOE_BLOCK = 256
TOKEN_TILE = 512
CODE_SHIFT = 65536.0
VMEM_LIMIT = 56 * 1024 * 1024


def _nt(a, b):
    return lax.dot_general(a, b, (((1,), (1,)), ((), ())), preferred_element_type=F32)


def _tn(a, b):
    return lax.dot_general(a, b, (((0,), (0,)), ((), ())), preferred_element_type=F32)


def _rows_to_tiles(ref, val, n_rows):
    for c in range(SUBLANES):
        ref[pl.ds(c, n_rows, stride=SUBLANES), :] = val[:, c * LANES:(c + 1) * LANES]


def _tiles_to_rows(ref, n_rows):
    return jnp.concatenate(
        [ref[pl.ds(c, n_rows, stride=SUBLANES), :] for c in range(SUBLANES)], axis=-1)


def _split_hi_lo(a):
    hi = a.astype(BF16)
    return hi, (a - hi.astype(F32)).astype(BF16)


def _stack_hi_lo(w):
    hi, lo = _split_hi_lo(w)
    return jnp.concatenate([jnp.concatenate([hi, lo], axis=-1),
                            jnp.concatenate([hi, jnp.zeros_like(lo)], axis=-1)], axis=-2)


def _col_to_lanes(col, n):
    r = lax.broadcasted_iota(I32, (LANES, LANES), 0)
    c = lax.broadcasted_iota(I32, (LANES, LANES), 1)
    eye = r == c
    rows = [jnp.sum(jnp.where(eye, col[g * LANES:(g + 1) * LANES, :], 0.0), axis=0, keepdims=True)
            for g in range(n // LANES)]
    return jnp.concatenate(rows, axis=0)


def _ada_kernel(c_ref, w_ref, b_ref, o_ref):
    c = c_ref[...]
    c_act = c * jax.nn.sigmoid(c)
    o_ref[0] = jnp.dot(c_act, w_ref[0], precision=HIGHEST, preferred_element_type=F32) + b_ref[0]


def _ada(c, w_ada, b_ada, tn=1024):
    depth, d, n = w_ada.shape
    bsz = c.shape[0]
    return pl.pallas_call(
        _ada_kernel,
        grid=(depth, n // tn),
        in_specs=[
            pl.BlockSpec((bsz, d), lambda l, j: (0, 0)),
            pl.BlockSpec((1, d, tn), lambda l, j: (l, 0, j)),
            pl.BlockSpec((1, 1, tn), lambda l, j: (l, 0, j)),
        ],
        out_specs=pl.BlockSpec((1, bsz, tn), lambda l, j: (l, 0, j)),
        out_shape=jax.ShapeDtypeStruct((depth, bsz, n), F32),
        compiler_params=pltpu.CompilerParams(
            dimension_semantics=("arbitrary", "arbitrary"), vmem_limit_bytes=VMEM_LIMIT),
        name="ada_mod",
    )(c, w_ada, b_ada.reshape(depth, 1, n))


def _gather_rows_start(d_ref, base, ypad_ref, buf0, buf1, sem0, sem1, *, n, unroll):
    def row_copy(r, src, buf, sem):
        return pltpu.make_async_copy(
            ypad_ref.at[pl.ds(pl.multiple_of(src * SUBLANES, SUBLANES), SUBLANES)],
            buf.at[pl.ds(pl.multiple_of(r * SUBLANES, SUBLANES), SUBLANES)], sem)

    def issue(jj, carry):
        for uu in range(unroll):
            r = jj * unroll + uu
            row_copy(r, d_ref[base + r], buf0, sem0).start()
            row_copy(r, d_ref[base + n + r], buf1, sem1).start()
        return carry

    if unroll == n:
        issue(0, 0)
    else:
        lax.fori_loop(0, n // unroll, issue, 0)


def _gather_rows_wait(ypad_ref, buf0, buf1, sem0, sem1, *, n):
    pltpu.make_async_copy(ypad_ref.at[pl.ds(0, n * SUBLANES)], buf0, sem0).wait()
    pltpu.make_async_copy(ypad_ref.at[pl.ds(0, n * SUBLANES)], buf1, sem1).wait()


def _gathered_rows_pipeline(i, n_tiles, d_ref, ypad_ref, y0buf, y1buf, sems, *, n, unroll, consume):
    start = functools.partial(_gather_rows_start, d_ref, n=n, unroll=unroll)

    @pl.when(i == 0)
    def _():
        start(0, ypad_ref, y0buf.at[0], y1buf.at[0], sems.at[0, 0], sems.at[0, 1])

    for p in range(2):
        @pl.when(i % 2 == p)
        def _(p=p):
            _gather_rows_wait(ypad_ref, y0buf.at[p], y1buf.at[p], sems.at[p, 0], sems.at[p, 1], n=n)

            @pl.when(i + 1 < n_tiles)
            def _():
                q = 1 - p
                start((i + 1) * (2 * n), ypad_ref, y0buf.at[q], y1buf.at[q],
                      sems.at[q, 0], sems.at[q, 1])

            consume(_tiles_to_rows(y0buf.at[p], n), _tiles_to_rows(y1buf.at[p], n))


def _mixer_kernel(*refs, ts, d, fused, unroll):
    if fused:
        d_ref, x_ref, gatep_ref, modp_ref, ypad_ref = refs[:5]
        refs = refs[5:]
        xin_scr, y0buf, y1buf, sem0, sem1 = refs[-5:]
        refs = refs[:-5]
    else:
        x_ref = refs[0]
        refs = refs[1:]
    (mod_ref, g1_ref, win_ref, avg_ref, ws_ref, bsb_ref, wal_ref, bal_ref, og_ref, wout_ref, g2_ref,
     wr_ref, br_ref, xo_ref, h2_ref, gate_ref, code_ref, cnt_ref,
     st_scr, proj_scr, la_scr, o_scr, cnt_scr) = refs
    a_w = d // 2
    kw = d // 4
    vw = d // 2
    hk = kw // B_HEADS
    hv = vw // B_HEADS
    o_q, o_k, o_v, o_r, o_a = 2 * a_w, 2 * a_w + kw, 2 * a_w + 2 * kw, 2 * a_w + 2 * kw + vw, \
        2 * a_w + 2 * kw + 2 * vw

    b = pl.program_id(0)
    s = pl.program_id(1)

    @pl.when(s == 0)
    def _():
        st_scr[...] = jnp.zeros_like(st_scr)

    @pl.when(jnp.logical_and(b == 0, s == 0))
    def _():
        cnt_scr[...] = jnp.zeros_like(cnt_scr)

    if fused:
        i = b * pl.num_programs(1) + s
        n_tiles = pl.num_programs(0) * pl.num_programs(1)

        @pl.when(i == 0)
        def _():
            _gather_rows_start(d_ref, 0, ypad_ref, y0buf.at[0], y1buf.at[0], sem0.at[0], sem1.at[0],
                               n=ts, unroll=unroll)

        for p in range(2):
            @pl.when(i % 2 == p)
            def _(p=p):
                _gather_rows_wait(ypad_ref, y0buf.at[p], y1buf.at[p], sem0.at[p], sem1.at[p], n=ts)
                gate_p = gatep_ref[...]
                moe = _tiles_to_rows(y0buf.at[p], ts) * gate_p[:, 0:1] \
                    + _tiles_to_rows(y1buf.at[p], ts) * gate_p[:, 1:2]
                xin_scr[...] = x_ref[...] + modp_ref[0][5:6] * moe

        x = xin_scr[...]
        nslot = (i + 1) % 2
        _gather_rows_start(d_ref, jnp.minimum(i + 1, n_tiles - 1) * (2 * ts), ypad_ref,
                           y0buf.at[nslot], y1buf.at[nslot], sem0.at[nslot], sem1.at[nslot],
                           n=ts, unroll=ts)
    else:
        x = x_ref[...]
    mod = mod_ref[0]
    sh1, sc1, g1 = mod[0:1], mod[1:2], mod[2:3]
    sh2, sc2 = mod[3:4], mod[4:5]

    h = x * lax.rsqrt(jnp.mean(x * x, axis=-1, keepdims=True) + EPS) * g1_ref[...]
    h = h * (1.0 + sc1) + sh1
    proj_scr[...] = jnp.dot(h.astype(BF16), win_ref[...], preferred_element_type=F32)

    u = jax.nn.gelu(proj_scr[:, 0:a_w])
    v = jax.nn.gelu(proj_scr[:, a_w:2 * a_w])
    vn = v * lax.rsqrt(jnp.mean(v * v, axis=-1, keepdims=True) + EPS) * avg_ref[...]
    vn = vn.astype(BF16)
    ahd = a_w // A_HEADS
    z_rows = []
    for ci in range(ts // A_CHUNK):
        z_heads = []
        for hd in range(A_HEADS):
            vc = vn[ci * A_CHUNK:(ci + 1) * A_CHUNK, hd * ahd:(hd + 1) * ahd]
            z_heads.append(jnp.dot(ws_ref[hd], vc, preferred_element_type=F32) + bsb_ref[hd])
        z_rows.append(jnp.concatenate(z_heads, axis=-1))
    y_a = u * jnp.concatenate(z_rows, axis=0)

    a_hi, a_lo = _split_hi_lo(proj_scr[:, o_a:o_a + LANES])
    xg2 = jnp.dot(jnp.concatenate([a_hi, a_lo], axis=-1), wal_ref[...], preferred_element_type=F32)
    xg = xg2[:, :kw] + xg2[:, kw:] + bal_ref[...]
    la_scr[...] = (jnp.minimum(xg, 0.0) - jnp.log(1.0 + jnp.exp(-jnp.abs(xg)))) / B_GATE_TAU

    c = GLA_CHUNK
    row = lax.broadcasted_iota(I32, (c, c), 0)
    col = lax.broadcasted_iota(I32, (c, c), 1)
    causal = col <= row
    tri = causal.astype(BF16)
    klane = lax.broadcasted_iota(I32, (1, kw), 1)
    bd_r = lax.broadcasted_iota(I32, (vw, kw), 0) // hv
    bd_c = lax.broadcasted_iota(I32, (vw, kw), 1) // hk
    bd_mask = (bd_r == bd_c).astype(F32)
    mid = c // 2 - 1

    st = st_scr[...]
    for j in range(ts // c):
        r0 = j * c
        g = la_scr[r0:r0 + c, :]
        q = proj_scr[r0:r0 + c, o_q:o_q + kw] * (hk ** -0.5)
        k = proj_scr[r0:r0 + c, o_k:o_k + kw]
        vv = proj_scr[r0:r0 + c, o_v:o_v + vw].astype(BF16)
        gp1 = g.astype(BF16)
        gp2, gp3 = _split_hi_lo(g - gp1.astype(F32))
        b3 = jnp.dot(tri, jnp.concatenate([gp1, gp2, gp3], axis=-1), preferred_element_type=F32)
        bcum = b3[:, :kw] + b3[:, kw:2 * kw] + b3[:, 2 * kw:]
        blast = bcum[c - 1:c, :]
        bmid = bcum[mid:mid + 1, :]
        qe = (q * jnp.exp(bcum)).astype(BF16)
        qt = q * jnp.exp(bcum - bmid)
        kt = (k * jnp.exp(bmid - bcum)).astype(BF16)
        kd = (k * jnp.exp(blast - bcum)).astype(BF16)
        o_inter = _nt(qe, st.astype(BF16))
        o_heads = []
        for hd in range(B_HEADS):
            m_h = jnp.logical_and(klane >= hd * hk, klane < (hd + 1) * hk)
            a_h = _nt(jnp.where(m_h, qt, 0.0).astype(BF16), kt)
            a_h = jnp.where(causal, a_h, 0.0).astype(BF16)
            o_heads.append(jnp.dot(a_h, vv[:, hd * hv:(hd + 1) * hv], preferred_element_type=F32))
        o_scr[r0:r0 + c, :] = o_inter + jnp.concatenate(o_heads, axis=-1)
        st = st * jnp.exp(blast) + _tn(vv, kd) * bd_mask
    st_scr[...] = st

    o = o_scr[...]
    og = og_ref[...]
    yb_heads = []
    for hd in range(B_HEADS):
        oh = o[:, hd * hv:(hd + 1) * hv]
        oh = oh * lax.rsqrt(jnp.mean(oh * oh, axis=-1, keepdims=True) + EPS)
        yb_heads.append(oh * og[:, hd * hv:(hd + 1) * hv])
    r = proj_scr[:, o_r:o_r + vw]
    y_b = jnp.concatenate(yb_heads, axis=-1) * (r * jax.nn.sigmoid(r))

    mixed = jnp.dot(jnp.concatenate([y_a, y_b], axis=-1).astype(BF16), wout_ref[...],
                    preferred_element_type=F32)
    xn = x + g1 * mixed
    xo_ref[...] = xn

    h2 = xn * lax.rsqrt(jnp.mean(xn * xn, axis=-1, keepdims=True) + EPS) * g2_ref[...]
    h2 = h2 * (1.0 + sc2) + sh2
    _rows_to_tiles(h2_ref, h2, ts)

    h2h, h2l = _split_hi_lo(h2)
    lg2 = jnp.dot(jnp.concatenate([h2h, h2l], axis=-1), wr_ref[...], preferred_element_type=F32)
    logits = lg2[:, :LANES] + lg2[:, LANES:] + br_ref[...]
    lane = lax.broadcasted_iota(I32, (ts, LANES), 1).astype(F32)
    neg = -jnp.inf
    big = float(LANES)
    is_g = lane < N_GROUPS
    gl = jnp.where(is_g, logits, neg)
    gmax = jnp.max(gl, axis=-1, keepdims=True)
    grp = jnp.min(jnp.where(gl == gmax, lane, big), axis=-1, keepdims=True)
    gsum = jnp.sum(jnp.where(is_g, jnp.exp(logits - gmax), 0.0), axis=-1, keepdims=True)
    p_grp = 1.0 / gsum
    lo = N_GROUPS + EXPERTS_PER_GROUP * grp
    el = jnp.where(jnp.logical_and(lane >= lo, lane < lo + EXPERTS_PER_GROUP), logits, neg)
    t1 = jnp.max(el, axis=-1, keepdims=True)
    i1 = jnp.min(jnp.where(el == t1, lane, big), axis=-1, keepdims=True)
    el2 = jnp.where(lane == i1, neg, el)
    t2 = jnp.max(el2, axis=-1, keepdims=True)
    i2 = jnp.min(jnp.where(el2 == t2, lane, big), axis=-1, keepdims=True)
    e2 = jnp.exp(t2 - t1)
    w0 = (1.0 / (1.0 + e2)) * p_grp
    w1 = (e2 / (1.0 + e2)) * p_grp
    e0 = i1 - N_GROUPS
    e1 = i2 - N_GROUPS

    sel0 = lane == e0
    sel1 = lane == e1
    onehot = jnp.logical_or(sel0, sel1)
    rr = lax.broadcasted_iota(I32, (ts, ts), 0)
    cc = lax.broadcasted_iota(I32, (ts, ts), 1)
    before = (cc < rr).astype(BF16)
    prefix = jnp.dot(before, onehot.astype(BF16), preferred_element_type=F32) + cnt_scr[0:1, :]
    rank0 = jnp.sum(jnp.where(sel0, prefix, 0.0), axis=-1, keepdims=True)
    rank1 = jnp.sum(jnp.where(sel1, prefix, 0.0), axis=-1, keepdims=True)
    cnt_new = cnt_scr[0:1, :] + jnp.sum(onehot.astype(F32), axis=0, keepdims=True)
    cnt_scr[...] = jnp.broadcast_to(cnt_new, cnt_scr.shape)
    cnt_ref[...] = jnp.broadcast_to(cnt_new, cnt_ref.shape)

    gate_ref[...] = jnp.where(lane == 0, w0, jnp.where(lane == 1, w1, 0.0))
    code_ref[...] = jnp.concatenate(
        [_col_to_lanes(e0 * CODE_SHIFT + rank0, ts), _col_to_lanes(e1 * CODE_SHIFT + rank1, ts)],
        axis=0).astype(I32)

    if fused:
        @pl.when(i == n_tiles - 1)
        def _():
            _gather_rows_wait(ypad_ref, y0buf.at[nslot], y1buf.at[nslot], sem0.at[nslot],
                              sem1.at[nslot], n=ts)


def _mixer(l, x2, mod, g1, win, avg, ws, bsb, wal, bal, og, wout, g2, wr, br, *, bsz, seq, ts,
           moe_prev=None, unroll=8):
    t, d = x2.shape
    n_s = seq // ts
    p_in = win.shape[-1]
    kw, vw = d // 4, d // 2
    lay2 = lambda b, s, *_: (l, 0, 0)
    lay3 = lambda b, s, *_: (l, 0, 0, 0)
    tok = lambda b, s, *_: (b * n_s + s, 0)
    code_rows = 2 * ts // LANES
    fused = moe_prev is not None
    kern = functools.partial(_mixer_kernel, ts=ts, d=d, fused=fused, unroll=unroll)
    x_specs = [pl.BlockSpec((ts, d), tok)]
    x_args = [x2]
    extra_scratch = []
    prefetch = []
    if fused:
        dflat, gate_prev, ypad_prev = moe_prev
        prefetch = [dflat]
        x_specs += [
            pl.BlockSpec((ts, LANES), tok),
            pl.BlockSpec((None, 1, 6, d), lambda b, s, *_: (l - 1, b, 0, 0)),
            pl.BlockSpec(memory_space=pl.ANY),
        ]
        x_args += [gate_prev, mod, ypad_prev]
        extra_scratch = [
            pltpu.VMEM((ts, d), F32),
            pltpu.VMEM((2, ts * SUBLANES, LANES), F32),
            pltpu.VMEM((2, ts * SUBLANES, LANES), F32),
            pltpu.SemaphoreType.DMA((2,)),
            pltpu.SemaphoreType.DMA((2,)),
        ]
    grid_spec = pltpu.PrefetchScalarGridSpec(
        num_scalar_prefetch=len(prefetch),
        grid=(bsz, n_s),
        in_specs=x_specs + [
            pl.BlockSpec((None, 1, 6, d), lambda b, s, *_: (l, b, 0, 0)),
            pl.BlockSpec((None, 1, d), lay2),
            pl.BlockSpec((None, d, p_in), lay2),
            pl.BlockSpec((None, 1, d // 2), lay2),
            pl.BlockSpec((None, A_HEADS, A_CHUNK, A_CHUNK), lay3),
            pl.BlockSpec((None, A_HEADS, A_CHUNK, A_CHUNK), lay3),
            pl.BlockSpec((None, 2 * LANES, 2 * kw), lay2),
            pl.BlockSpec((None, 1, kw), lay2),
            pl.BlockSpec((None, 1, vw), lay2),
            pl.BlockSpec((None, d, d), lay2),
            pl.BlockSpec((None, 1, d), lay2),
            pl.BlockSpec((None, 2 * d, 2 * LANES), lay2),
            pl.BlockSpec((None, 1, LANES), lay2),
        ],
        out_specs=[
            pl.BlockSpec((ts, d), tok),
            pl.BlockSpec((ts * SUBLANES, LANES), tok),
            pl.BlockSpec((ts, LANES), tok),
            pl.BlockSpec((None, code_rows, LANES), lambda b, s, *_: (b * n_s + s, 0, 0)),
            pl.BlockSpec((SUBLANES, LANES), lambda b, s, *_: (0, 0)),
        ],
        scratch_shapes=[
            pltpu.VMEM((vw, kw), F32),
            pltpu.VMEM((ts, p_in), F32),
            pltpu.VMEM((ts, kw), F32),
            pltpu.VMEM((ts, vw), F32),
            pltpu.VMEM((SUBLANES, LANES), F32),
        ] + extra_scratch,
    )
    return pl.pallas_call(
        kern,
        grid_spec=grid_spec,
        out_shape=[
            jax.ShapeDtypeStruct((t, d), F32),
            jax.ShapeDtypeStruct((t * SUBLANES, LANES), F32),
            jax.ShapeDtypeStruct((t, LANES), F32),
            jax.ShapeDtypeStruct((t // ts, code_rows, LANES), I32),
            jax.ShapeDtypeStruct((SUBLANES, LANES), F32),
        ],
        compiler_params=pltpu.CompilerParams(
            dimension_semantics=("arbitrary", "arbitrary"), vmem_limit_bytes=VMEM_LIMIT),
        name="mixer",
    )(*prefetch, *x_args, mod, g1, win, avg, ws, bsb, wal, bal, og, wout, g2, wr, br)


def _slots_kernel(code_ref, cnt_ref, d_ref, be_ref, bv_ref, *, blk):
    cnt = cnt_ref[0:1, :]
    padded = jnp.floor((cnt + (blk - 1.0)) / blk) * blk
    r = lax.broadcasted_iota(I32, (LANES, LANES), 0)
    c = lax.broadcasted_iota(I32, (LANES, LANES), 1)
    upper = (r <= c).astype(F32)
    pad_end = jnp.dot(jnp.broadcast_to(padded, (SUBLANES, LANES)), upper, precision=HIGHEST,
                      preferred_element_type=F32)[0:1, :]
    pad_start = pad_end - padded

    code = code_ref[...].astype(F32)
    e = jnp.floor(code / CODE_SHIFT)
    slot = code - e * CODE_SHIFT
    for k in range(N_EXPERTS):
        slot = slot + jnp.where(e == k, pad_start[:, k:k + 1], 0.0)
    d_ref[...] = slot.astype(I32)

    shp = be_ref.shape
    jpos = (lax.broadcasted_iota(I32, shp, 0) * LANES + lax.broadcasted_iota(I32, shp, 1)).astype(F32) * blk
    be = jnp.zeros(shp, F32)
    for k in range(N_EXPERTS):
        be = be + jnp.where(pad_end[:, k:k + 1] <= jpos, 1.0, 0.0)
    be = jnp.minimum(be, N_EXPERTS - 1.0)
    seg_end = jnp.zeros(shp, F32)
    for k in range(N_EXPERTS):
        seg_end = seg_end + jnp.where(be == k, pad_start[:, k:k + 1] + cnt[:, k:k + 1], 0.0)
    be_ref[...] = be.astype(I32)
    bv_ref[...] = jnp.clip(seg_end - jpos, 0.0, float(blk)).astype(I32)


def _slots(codes2, counts, *, blk):
    kern = functools.partial(_slots_kernel, blk=blk)
    return pl.pallas_call(
        kern,
        out_shape=[
            jax.ShapeDtypeStruct(codes2.shape, I32),
            jax.ShapeDtypeStruct((SUBLANES, LANES), I32),
            jax.ShapeDtypeStruct((SUBLANES, LANES), I32),
        ],
        compiler_params=pltpu.CompilerParams(vmem_limit_bytes=VMEM_LIMIT),
        name="slots",
    )(codes2, counts)


def _dispatch_kernel(d_ref, bv_ref, h2_ref, xpad_ref, zero_scr, sem0, sem1, semz,
                     *, td, unroll, blk, n_blocks, blocks_per_step):
    i = pl.program_id(0)
    base = i * (2 * td)

    @pl.when(i == 0)
    def _():
        zero_scr[...] = jnp.zeros_like(zero_scr)

    def row_copy(r, dst, sem):
        return pltpu.make_async_copy(
            h2_ref.at[pl.ds(pl.multiple_of(r * SUBLANES, SUBLANES), SUBLANES)],
            xpad_ref.at[pl.ds(pl.multiple_of(dst * SUBLANES, SUBLANES), SUBLANES)], sem)

    def issue(jj, carry):
        for uu in range(unroll):
            r = jj * unroll + uu
            row_copy(r, d_ref[base + r], sem0).start()
            row_copy(r, d_ref[base + td + r], sem1).start()
        return carry

    lax.fori_loop(0, td // unroll, issue, 0)

    def pad_fill(act):
        for jb in range(blocks_per_step):
            j = i * blocks_per_step + jb
            jc = jnp.minimum(j, n_blocks - 1)
            valid = bv_ref[jc]
            n_pad = jnp.where(j < n_blocks, blk - valid, 0)
            row = jc * blk + valid
            size = blk
            while size >= 1:
                hit = (n_pad & size) != 0

                @pl.when(hit)
                def _(row=row, size=size):
                    cp = pltpu.make_async_copy(
                        zero_scr.at[pl.ds(0, size * SUBLANES)],
                        xpad_ref.at[pl.ds(pl.multiple_of(row * SUBLANES, SUBLANES), size * SUBLANES)],
                        semz)
                    act(cp)

                row = row + jnp.where(hit, size, 0)
                size //= 2

    pad_fill(lambda cp: cp.start())

    def drain(jj, carry):
        for uu in range(unroll):
            r = jj * unroll + uu
            row_copy(r, d_ref[base + r], sem0).wait()
            row_copy(r, d_ref[base + td + r], sem1).wait()
        return carry

    lax.fori_loop(0, td // unroll, drain, 0)
    pad_fill(lambda cp: cp.wait())


def _dispatch(dflat, block_valid, h2t, *, n_blocks, blk, td, unroll=8):
    t = h2t.shape[0] // SUBLANES
    n_steps = t // td
    blocks_per_step = (n_blocks + n_steps - 1) // n_steps
    kern = functools.partial(_dispatch_kernel, td=td, unroll=unroll, blk=blk, n_blocks=n_blocks,
                             blocks_per_step=blocks_per_step)
    return pl.pallas_call(
        kern,
        grid_spec=pltpu.PrefetchScalarGridSpec(
            num_scalar_prefetch=2,
            grid=(n_steps,),
            in_specs=[pl.BlockSpec((td * SUBLANES, LANES), lambda i, dd, bv: (i, 0))],
            out_specs=pl.BlockSpec(memory_space=pl.ANY),
            scratch_shapes=[
                pltpu.VMEM((blk * SUBLANES, LANES), F32),
                pltpu.SemaphoreType.DMA, pltpu.SemaphoreType.DMA, pltpu.SemaphoreType.DMA],
        ),
        out_shape=jax.ShapeDtypeStruct((n_blocks * blk * SUBLANES, LANES), F32),
        compiler_params=pltpu.CompilerParams(
            dimension_semantics=("arbitrary",), vmem_limit_bytes=VMEM_LIMIT),
        name="dispatch",
    )(dflat, block_valid, h2t)


def _experts_kernel(be_ref, bv_ref, x_ref, w1_ref, w3_ref, w2_ref, y_ref, w1b, w3b, w2b, *, blk):
    j = pl.program_id(0)
    valid = bv_ref[j]
    used = valid > 0
    prev = be_ref[jnp.maximum(j - 1, 0)]
    fresh = jnp.logical_or(j == 0, be_ref[j] != prev)

    @pl.when(jnp.logical_and(used, fresh))
    def _():
        w1b[...] = w1_ref[...].astype(BF16)
        w3b[...] = w3_ref[...].astype(BF16)
        w2b[...] = w2_ref[...].astype(BF16)

    @pl.when(used)
    def _():
        x = _tiles_to_rows(x_ref, blk).astype(BF16)
        a = jnp.dot(x, w1b[...], preferred_element_type=F32)
        g = jnp.dot(x, w3b[...], preferred_element_type=F32)
        hmid = ((a * jax.nn.sigmoid(a)) * g).astype(BF16)
        y = jnp.dot(hmid, w2b[...], preferred_element_type=F32)
        _rows_to_tiles(y_ref, y, blk)

    @pl.when(jnp.logical_not(used))
    def _():
        y_ref[...] = jnp.zeros_like(y_ref)


def _experts(l, block_expert, block_valid, xpad, w1, w3, w2, *, n_blocks, blk):
    ne, d, de = w1.shape[1:]
    w1 = w1.reshape(-1, d, de)
    w3 = w3.reshape(-1, d, de)
    w2 = w2.reshape(-1, de, d)
    wmap = lambda j, be, bv: (l * ne + be[j], 0, 0)
    kern = functools.partial(_experts_kernel, blk=blk)
    return pl.pallas_call(
        kern,
        grid_spec=pltpu.PrefetchScalarGridSpec(
            num_scalar_prefetch=2,
            grid=(n_blocks,),
            in_specs=[
                pl.BlockSpec((blk * SUBLANES, LANES), lambda j, be, bv: (j, 0)),
                pl.BlockSpec((None, d, de), wmap),
                pl.BlockSpec((None, d, de), wmap),
                pl.BlockSpec((None, de, d), wmap),
            ],
            out_specs=pl.BlockSpec((blk * SUBLANES, LANES), lambda j, be, bv: (j, 0)),
            scratch_shapes=[
                pltpu.VMEM((d, de), BF16),
                pltpu.VMEM((d, de), BF16),
                pltpu.VMEM((de, d), BF16),
            ],
        ),
        out_shape=jax.ShapeDtypeStruct(xpad.shape, F32),
        compiler_params=pltpu.CompilerParams(
            dimension_semantics=("arbitrary",), vmem_limit_bytes=VMEM_LIMIT),
        name="experts",
    )(block_expert, block_valid, xpad, w1, w3, w2)


def _combine_kernel(d_ref, x_ref, gate_ref, mod_ref, fg_ref, ypad_ref, o_ref,
                    y0buf, y1buf, sems, *, tc, unroll):
    def consume(y0, y1):
        gate = gate_ref[...]
        xo = x_ref[...] + mod_ref[0][5:6] * (y0 * gate[:, 0:1] + y1 * gate[:, 1:2])
        o_ref[...] = xo * lax.rsqrt(jnp.mean(xo * xo, axis=-1, keepdims=True) + EPS) * fg_ref[...]

    _gathered_rows_pipeline(pl.program_id(0), pl.num_programs(0), d_ref, ypad_ref, y0buf, y1buf, sems,
                            n=tc, unroll=unroll, consume=consume)


def _combine(l, dflat, x2, gate, mod, fgain, ypad, *, seq, tc, unroll=8):
    t, d = x2.shape
    per_b = seq // tc
    kern = functools.partial(_combine_kernel, tc=tc, unroll=unroll)
    return pl.pallas_call(
        kern,
        grid_spec=pltpu.PrefetchScalarGridSpec(
            num_scalar_prefetch=1,
            grid=(t // tc,),
            in_specs=[
                pl.BlockSpec((tc, d), lambda i, dd: (i, 0)),
                pl.BlockSpec((tc, LANES), lambda i, dd: (i, 0)),
                pl.BlockSpec((None, 1, 6, d), lambda i, dd: (l, i // per_b, 0, 0)),
                pl.BlockSpec((1, d), lambda i, dd: (0, 0)),
                pl.BlockSpec(memory_space=pl.ANY),
            ],
            out_specs=pl.BlockSpec((tc, d), lambda i, dd: (i, 0)),
            scratch_shapes=[
                pltpu.VMEM((2, tc * SUBLANES, LANES), F32),
                pltpu.VMEM((2, tc * SUBLANES, LANES), F32),
                pltpu.SemaphoreType.DMA((2, 2)),
            ],
        ),
        out_shape=jax.ShapeDtypeStruct((t, d), F32),
        compiler_params=pltpu.CompilerParams(
            dimension_semantics=("arbitrary",), vmem_limit_bytes=VMEM_LIMIT),
        name="combine",
    )(dflat, x2, gate, mod, fgain, ypad)


@jax.jit
def _forward(x, c, w_ada, b_ada, norm1_gain, norm2_gain, w_in, a_v_gain, a_w_s, a_b_s, b_w_alpha,
             b_b_alpha, b_out_gain, w_out, w_group, b_group, w_router, b_router, w1, w3, w2,
             final_gain):
    bsz, seq, d = x.shape
    depth = w_ada.shape[0]
    t = bsz * seq
    ts = min(TOKEN_TILE, seq)
    blk = MOE_BLOCK
    n_blocks = (2 * t + blk - 1) // blk + N_EXPERTS
    assert t <= CODE_SHIFT and n_blocks <= SUBLANES * LANES and seq % ts == 0 and ts % LANES == 0
    p_in = w_in.shape[-1]
    p_main = p_in - B_GATE_RANK
    kw = d // 4

    mod = _ada(c, w_ada, b_ada).reshape(depth, bsz, 6, d)

    win = jnp.concatenate(
        [w_in[..., :p_main],
         jnp.pad(w_in[..., p_main:], ((0, 0), (0, 0), (0, LANES - B_GATE_RANK)))], axis=-1).astype(BF16)
    wal = _stack_hi_lo(jnp.pad(b_w_alpha, ((0, 0), (0, LANES - B_GATE_RANK), (0, 0))))
    causal = jnp.tril(jnp.ones((A_CHUNK, A_CHUNK), dtype=bool))
    ws = jnp.where(causal[None, None], a_w_s, 0.0).astype(BF16)
    bsb = jnp.broadcast_to(a_b_s[..., None], a_b_s.shape + (d // 2 // A_HEADS,))
    wout = w_out.astype(BF16)
    n_r = N_GROUPS + N_EXPERTS
    wr = _stack_hi_lo(
        jnp.pad(jnp.concatenate([w_group, w_router], axis=-1), ((0, 0), (0, 0), (0, LANES - n_r))))
    br = jnp.pad(jnp.concatenate([b_group, b_router], axis=-1), ((0, 0), (0, LANES - n_r)))

    g1 = norm1_gain.reshape(depth, 1, d)
    g2 = norm2_gain.reshape(depth, 1, d)
    avg = a_v_gain.reshape(depth, 1, d // 2)
    bal = b_b_alpha.reshape(depth, 1, kw)
    og = b_out_gain.reshape(depth, 1, d // 2)
    br = br.reshape(depth, 1, LANES)
    fg = final_gain.reshape(1, d)

    x2 = x.reshape(t, d)
    moe_prev = None
    for l in range(depth):
        x2, h2t, gate, codes, counts = _mixer(
            l, x2, mod, g1, win, avg, ws, bsb, wal, bal, og, wout, g2, wr, br,
            bsz=bsz, seq=seq, ts=ts, moe_prev=moe_prev)
        slots, block_expert, block_valid = _slots(codes.reshape(-1, LANES), counts, blk=blk)
        dflat = slots.reshape(-1)
        block_valid = block_valid.reshape(-1)
        xpad = _dispatch(dflat, block_valid, h2t, n_blocks=n_blocks, blk=blk, td=ts)
        ypad = _experts(l, block_expert.reshape(-1), block_valid, xpad, w1, w3, w2,
                        n_blocks=n_blocks, blk=blk)
        moe_prev = (dflat, gate, ypad)
    x2 = _combine(depth - 1, dflat, x2, gate, mod, fg, ypad, seq=seq, tc=ts)
    return x2.reshape(bsz, seq, d)


def kernel(x, c, w_ada, b_ada, norm1_gain, norm2_gain, w_in, a_v_gain, a_w_s, a_b_s, b_w_alpha,
           b_b_alpha, b_out_gain, w_out, w_group, b_group, w_router, b_router, w1, w3, w2,
           final_gain):
    return _forward(x, c, w_ada, b_ada, norm1_gain, norm2_gain, w_in, a_v_gain, a_w_s, a_b_s,
                    b_w_alpha, b_b_alpha, b_out_gain, w_out, w_group, b_group, w_router, b_router,
                    w1, w3, w2, final_gain)
```

```python
import functools

import jax
import jax.numpy as jnp
from jax import lax
from jax.experimental import pallas as pl
from jax.experimental.pallas import tpu as pltpu

F32 = jnp.float32
BF16 = jnp.bfloat16
I32 = jnp.int32
HIGHEST = lax.Precision.HIGHEST

EPS = 1e-6
LANES = 128
SUBLANES = 8
A_HEADS = 4
A_CHUNK = 128
B_HEADS = 4
B_GATE_RANK = 16
B_GATE_TAU = 16.0
GLA_CHUNK = 128
N_GROUPS = 4
EXPERTS_PER_GROUP = 8
N_EXPERTS = N_GROUPS * EXPERTS_PER_GROUP
MOE_BLOCK = 512
TOKEN_TILE = 512
CODE_SHIFT = 65536.0
VMEM_LIMIT = 56 * 1024 * 1024


def _nt(a, b):
    return lax.dot_general(a, b, (((1,), (1,)), ((), ())), preferred_element_type=F32)


def _tn(a, b):
    return lax.dot_general(a, b, (((0,), (0,)), ((), ())), preferred_element_type=F32)


def _rows_to_tiles(ref, val, n_rows):
    for c in range(SUBLANES):
        ref[pl.ds(c, n_rows, stride=SUBLANES), :] = val[:, c * LANES:(c + 1) * LANES]


def _tiles_to_rows(ref, n_rows):
    return jnp.concatenate(
        [ref[pl.ds(c, n_rows, stride=SUBLANES), :] for c in range(SUBLANES)], axis=-1)


def _split_hi_lo(a):
    hi = a.astype(BF16)
    return hi, (a - hi.astype(F32)).astype(BF16)


def _stack_hi_lo(w):
    hi, lo = _split_hi_lo(w)
    return jnp.concatenate([jnp.concatenate([hi, lo], axis=-1),
                            jnp.concatenate([hi, jnp.zeros_like(lo)], axis=-1)], axis=-2)


def _col_to_lanes(col, n):
    r = lax.broadcasted_iota(I32, (LANES, LANES), 0)
    c = lax.broadcasted_iota(I32, (LANES, LANES), 1)
    eye = r == c
    rows = [jnp.sum(jnp.where(eye, col[g * LANES:(g + 1) * LANES, :], 0.0), axis=0, keepdims=True)
            for g in range(n // LANES)]
    return jnp.concatenate(rows, axis=0)


def _ada_kernel(c_ref, w_ref, b_ref, o_ref):
    c = c_ref[...]
    c_act = c * jax.nn.sigmoid(c)
    o_ref[0] = jnp.dot(c_act, w_ref[0], precision=HIGHEST, preferred_element_type=F32) + b_ref[0]


def _ada(c, w_ada, b_ada, tn=1024):
    depth, d, n = w_ada.shape
    bsz = c.shape[0]
    return pl.pallas_call(
        _ada_kernel,
        grid=(depth, n // tn),
        in_specs=[
            pl.BlockSpec((bsz, d), lambda l, j: (0, 0)),
            pl.BlockSpec((1, d, tn), lambda l, j: (l, 0, j)),
            pl.BlockSpec((1, 1, tn), lambda l, j: (l, 0, j)),
        ],
        out_specs=pl.BlockSpec((1, bsz, tn), lambda l, j: (l, 0, j)),
        out_shape=jax.ShapeDtypeStruct((depth, bsz, n), F32),
        compiler_params=pltpu.CompilerParams(
            dimension_semantics=("arbitrary", "arbitrary"), vmem_limit_bytes=VMEM_LIMIT),
        name="ada_mod",
    )(c, w_ada, b_ada.reshape(depth, 1, n))


def _gather_rows_start(d_ref, base, ypad_ref, buf0, buf1, sem0, sem1, *, n, unroll):
    def row_copy(r, src, buf, sem):
        return pltpu.make_async_copy(
            ypad_ref.at[pl.ds(pl.multiple_of(src * SUBLANES, SUBLANES), SUBLANES)],
            buf.at[pl.ds(pl.multiple_of(r * SUBLANES, SUBLANES), SUBLANES)], sem)

    def issue(jj, carry):
        for uu in range(unroll):
            r = jj * unroll + uu
            row_copy(r, d_ref[base + r], buf0, sem0).start(priority=0)
            row_copy(r, d_ref[base + n + r], buf1, sem1).start(priority=1)
        return carry

    if unroll == n:
        issue(0, 0)
    else:
        lax.fori_loop(0, n // unroll, issue, 0)


def _gather_rows_wait(ypad_ref, buf0, buf1, sem0, sem1, *, n):
    pltpu.make_async_copy(ypad_ref.at[pl.ds(0, n * SUBLANES)], buf0, sem0).wait()
    pltpu.make_async_copy(ypad_ref.at[pl.ds(0, n * SUBLANES)], buf1, sem1).wait()


def _gathered_rows_pipeline(i, n_tiles, d_ref, ypad_ref, y0buf, y1buf, sems, *, n, unroll, consume):
    start = functools.partial(_gather_rows_start, d_ref, n=n, unroll=unroll)

    @pl.when(i == 0)
    def _():
        start(0, ypad_ref, y0buf.at[0], y1buf.at[0], sems.at[0, 0], sems.at[0, 1])

    for p in range(2):
        @pl.when(i % 2 == p)
        def _(p=p):
            _gather_rows_wait(ypad_ref, y0buf.at[p], y1buf.at[p], sems.at[p, 0], sems.at[p, 1], n=n)

            @pl.when(i + 1 < n_tiles)
            def _():
                q = 1 - p
                start((i + 1) * (2 * n), ypad_ref, y0buf.at[q], y1buf.at[q],
                      sems.at[q, 0], sems.at[q, 1])

            consume(_tiles_to_rows(y0buf.at[p], n), _tiles_to_rows(y1buf.at[p], n))


def _mixer_kernel(*refs, ts, d, fused, unroll):
    if fused:
        d_ref, x_ref, gatep_ref, modp_ref, ypad_ref = refs[:5]
        refs = refs[5:]
        xin_scr, y0buf, y1buf, sem0, sem1 = refs[-5:]
        refs = refs[:-5]
    else:
        x_ref = refs[0]
        refs = refs[1:]
    (mod_ref, g1_ref, win_ref, avg_ref, ws_ref, bsb_ref, wal_ref, bal_ref, og_ref, wout_ref, g2_ref,
     wr_ref, br_ref, xo_ref, h2_ref, gate_ref, code_ref, cnt_ref,
     st_scr, proj_scr, la_scr, o_scr, cnt_scr) = refs
    a_w = d // 2
    kw = d // 4
    vw = d // 2
    hk = kw // B_HEADS
    hv = vw // B_HEADS
    o_q, o_k, o_v, o_r, o_a = 2 * a_w, 2 * a_w + kw, 2 * a_w + 2 * kw, 2 * a_w + 2 * kw + vw, \
        2 * a_w + 2 * kw + 2 * vw

    b = pl.program_id(0)
    s = pl.program_id(1)

    @pl.when(s == 0)
    def _():
        st_scr[...] = jnp.zeros_like(st_scr)

    @pl.when(jnp.logical_and(b == 0, s == 0))
    def _():
        cnt_scr[...] = jnp.zeros_like(cnt_scr)

    if fused:
        i = b * pl.num_programs(1) + s
        n_tiles = pl.num_programs(0) * pl.num_programs(1)

        @pl.when(i == 0)
        def _():
            _gather_rows_start(d_ref, 0, ypad_ref, y0buf.at[0], y1buf.at[0], sem0.at[0], sem1.at[0],
                               n=ts, unroll=unroll)

        for p in range(2):
            @pl.when(i % 2 == p)
            def _(p=p):
                _gather_rows_wait(ypad_ref, y0buf.at[p], y1buf.at[p], sem0.at[p], sem1.at[p], n=ts)
                gate_p = gatep_ref[...]
                moe = _tiles_to_rows(y0buf.at[p], ts) * gate_p[:, 0:1] \
                    + _tiles_to_rows(y1buf.at[p], ts) * gate_p[:, 1:2]
                xin_scr[...] = x_ref[...] + modp_ref[0][5:6] * moe

        x = xin_scr[...]
        nslot = (i + 1) % 2
        _gather_rows_start(d_ref, jnp.minimum(i + 1, n_tiles - 1) * (2 * ts), ypad_ref,
                           y0buf.at[nslot], y1buf.at[nslot], sem0.at[nslot], sem1.at[nslot],
                           n=ts, unroll=ts)
    else:
        x = x_ref[...]
    mod = mod_ref[0]
    sh1, sc1, g1 = mod[0:1], mod[1:2], mod[2:3]
    sh2, sc2 = mod[3:4], mod[4:5]

    h = x * lax.rsqrt(jnp.mean(x * x, axis=-1, keepdims=True) + EPS) * g1_ref[...]
    h = h * (1.0 + sc1) + sh1
    proj_scr[...] = jnp.dot(h.astype(BF16), win_ref[...], preferred_element_type=F32)

    u = jax.nn.gelu(proj_scr[:, 0:a_w])
    v = jax.nn.gelu(proj_scr[:, a_w:2 * a_w])
    vn = v * lax.rsqrt(jnp.mean(v * v, axis=-1, keepdims=True) + EPS) * avg_ref[...]
    vn = vn.astype(BF16)
    ahd = a_w // A_HEADS
    z_rows = []
    for ci in range(ts // A_CHUNK):
        z_heads = []
        for hd in range(A_HEADS):
            vc = vn[ci * A_CHUNK:(ci + 1) * A_CHUNK, hd * ahd:(hd + 1) * ahd]
            z_heads.append(jnp.dot(ws_ref[hd], vc, preferred_element_type=F32) + bsb_ref[hd])
        z_rows.append(jnp.concatenate(z_heads, axis=-1))
    y_a = u * jnp.concatenate(z_rows, axis=0)

    a_hi, a_lo = _split_hi_lo(proj_scr[:, o_a:o_a + LANES])
    xg2 = jnp.dot(jnp.concatenate([a_hi, a_lo], axis=-1), wal_ref[...], preferred_element_type=F32)
    xg = xg2[:, :kw] + xg2[:, kw:] + bal_ref[...]
    la_scr[...] = (jnp.minimum(xg, 0.0) - jnp.log(1.0 + jnp.exp(-jnp.abs(xg)))) / B_GATE_TAU

    c = GLA_CHUNK
    row = lax.broadcasted_iota(I32, (c, c), 0)
    col = lax.broadcasted_iota(I32, (c, c), 1)
    causal = col <= row
    tri = causal.astype(BF16)
    klane = lax.broadcasted_iota(I32, (1, kw), 1)
    bd_r = lax.broadcasted_iota(I32, (vw, kw), 0) // hv
    bd_c = lax.broadcasted_iota(I32, (vw, kw), 1) // hk
    bd_mask = (bd_r == bd_c).astype(F32)
    mid = c // 2 - 1

    st = st_scr[...]
    for j in range(ts // c):
        r0 = j * c
        g = la_scr[r0:r0 + c, :]
        q = proj_scr[r0:r0 + c, o_q:o_q + kw] * (hk ** -0.5)
        k = proj_scr[r0:r0 + c, o_k:o_k + kw]
        vv = proj_scr[r0:r0 + c, o_v:o_v + vw].astype(BF16)
        gp1 = g.astype(BF16)
        gp2, gp3 = _split_hi_lo(g - gp1.astype(F32))
        b3 = jnp.dot(tri, jnp.concatenate([gp1, gp2, gp3], axis=-1), preferred_element_type=F32)
        bcum = b3[:, :kw] + b3[:, kw:2 * kw] + b3[:, 2 * kw:]
        blast = bcum[c - 1:c, :]
        bmid = bcum[mid:mid + 1, :]
        qe = (q * jnp.exp(bcum)).astype(BF16)
        qt = q * jnp.exp(bcum - bmid)
        kt = (k * jnp.exp(bmid - bcum)).astype(BF16)
        kd = (k * jnp.exp(blast - bcum)).astype(BF16)
        o_inter = _nt(qe, st.astype(BF16))
        o_heads = []
        for hd in range(B_HEADS):
            m_h = jnp.logical_and(klane >= hd * hk, klane < (hd + 1) * hk)
            a_h = _nt(jnp.where(m_h, qt, 0.0).astype(BF16), kt)
            a_h = jnp.where(causal, a_h, 0.0).astype(BF16)
            o_heads.append(jnp.dot(a_h, vv[:, hd * hv:(hd + 1) * hv], preferred_element_type=F32))
        o_scr[r0:r0 + c, :] = o_inter + jnp.concatenate(o_heads, axis=-1)
        st = st * jnp.exp(blast) + _tn(vv, kd) * bd_mask
    st_scr[...] = st

    o = o_scr[...]
    og = og_ref[...]
    yb_heads = []
    for hd in range(B_HEADS):
        oh = o[:, hd * hv:(hd + 1) * hv]
        oh = oh * lax.rsqrt(jnp.mean(oh * oh, axis=-1, keepdims=True) + EPS)
        yb_heads.append(oh * og[:, hd * hv:(hd + 1) * hv])
    r = proj_scr[:, o_r:o_r + vw]
    y_b = jnp.concatenate(yb_heads, axis=-1) * (r * jax.nn.sigmoid(r))

    mixed = jnp.dot(jnp.concatenate([y_a, y_b], axis=-1).astype(BF16), wout_ref[...],
                    preferred_element_type=F32)
    xn = x + g1 * mixed
    xo_ref[...] = xn

    h2 = xn * lax.rsqrt(jnp.mean(xn * xn, axis=-1, keepdims=True) + EPS) * g2_ref[...]
    h2 = h2 * (1.0 + sc2) + sh2
    _rows_to_tiles(h2_ref, h2, ts)

    h2h, h2l = _split_hi_lo(h2)
    lg2 = jnp.dot(jnp.concatenate([h2h, h2l], axis=-1), wr_ref[...], preferred_element_type=F32)
    logits = lg2[:, :LANES] + lg2[:, LANES:] + br_ref[...]
    lane = lax.broadcasted_iota(I32, (ts, LANES), 1).astype(F32)
    neg = -jnp.inf
    big = float(LANES)
    is_g = lane < N_GROUPS
    gl = jnp.where(is_g, logits, neg)
    gmax = jnp.max(gl, axis=-1, keepdims=True)
    grp = jnp.min(jnp.where(gl == gmax, lane, big), axis=-1, keepdims=True)
    gsum = jnp.sum(jnp.where(is_g, jnp.exp(logits - gmax), 0.0), axis=-1, keepdims=True)
    p_grp = 1.0 / gsum
    lo = N_GROUPS + EXPERTS_PER_GROUP * grp
    el = jnp.where(jnp.logical_and(lane >= lo, lane < lo + EXPERTS_PER_GROUP), logits, neg)
    t1 = jnp.max(el, axis=-1, keepdims=True)
    i1 = jnp.min(jnp.where(el == t1, lane, big), axis=-1, keepdims=True)
    el2 = jnp.where(lane == i1, neg, el)
    t2 = jnp.max(el2, axis=-1, keepdims=True)
    i2 = jnp.min(jnp.where(el2 == t2, lane, big), axis=-1, keepdims=True)
    e2 = jnp.exp(t2 - t1)
    w0 = (1.0 / (1.0 + e2)) * p_grp
    w1 = (e2 / (1.0 + e2)) * p_grp
    e0 = i1 - N_GROUPS
    e1 = i2 - N_GROUPS

    sel0 = lane == e0
    sel1 = lane == e1
    onehot = jnp.logical_or(sel0, sel1)
    rr = lax.broadcasted_iota(I32, (ts, ts), 0)
    cc = lax.broadcasted_iota(I32, (ts, ts), 1)
    before = (cc < rr).astype(BF16)
    prefix = jnp.dot(before, onehot.astype(BF16), preferred_element_type=F32) + cnt_scr[0:1, :]
    rank0 = jnp.sum(jnp.where(sel0, prefix, 0.0), axis=-1, keepdims=True)
    rank1 = jnp.sum(jnp.where(sel1, prefix, 0.0), axis=-1, keepdims=True)
    cnt_new = cnt_scr[0:1, :] + jnp.sum(onehot.astype(F32), axis=0, keepdims=True)
    cnt_scr[...] = jnp.broadcast_to(cnt_new, cnt_scr.shape)
    cnt_ref[...] = jnp.broadcast_to(cnt_new, cnt_ref.shape)

    gate_ref[...] = jnp.where(lane == 0, w0, jnp.where(lane == 1, w1, 0.0))
    code_ref[...] = jnp.concatenate(
        [_col_to_lanes(e0 * CODE_SHIFT + rank0, ts), _col_to_lanes(e1 * CODE_SHIFT + rank1, ts)],
        axis=0).astype(I32)

    if fused:
        @pl.when(i == n_tiles - 1)
        def _():
            _gather_rows_wait(ypad_ref, y0buf.at[nslot], y1buf.at[nslot], sem0.at[nslot],
                              sem1.at[nslot], n=ts)


def _mixer(l, x2, mod, g1, win, avg, ws, bsb, wal, bal, og, wout, g2, wr, br, *, bsz, seq, ts,
           moe_prev=None, unroll=8):
    t, d = x2.shape
    n_s = seq // ts
    p_in = win.shape[-1]
    kw, vw = d // 4, d // 2
    lay2 = lambda b, s, *_: (l, 0, 0)
    lay3 = lambda b, s, *_: (l, 0, 0, 0)
    tok = lambda b, s, *_: (b * n_s + s, 0)
    code_rows = 2 * ts // LANES
    fused = moe_prev is not None
    kern = functools.partial(_mixer_kernel, ts=ts, d=d, fused=fused, unroll=unroll)
    x_specs = [pl.BlockSpec((ts, d), tok)]
    x_args = [x2]
    extra_scratch = []
    prefetch = []
    if fused:
        dflat, gate_prev, ypad_prev = moe_prev
        prefetch = [dflat]
        x_specs += [
            pl.BlockSpec((ts, LANES), tok),
            pl.BlockSpec((None, 1, 6, d), lambda b, s, *_: (l - 1, b, 0, 0)),
            pl.BlockSpec(memory_space=pl.ANY),
        ]
        x_args += [gate_prev, mod, ypad_prev]
        extra_scratch = [
            pltpu.VMEM((ts, d), F32),
            pltpu.VMEM((2, ts * SUBLANES, LANES), F32),
            pltpu.VMEM((2, ts * SUBLANES, LANES), F32),
            pltpu.SemaphoreType.DMA((2,)),
            pltpu.SemaphoreType.DMA((2,)),
        ]
    grid_spec = pltpu.PrefetchScalarGridSpec(
        num_scalar_prefetch=len(prefetch),
        grid=(bsz, n_s),
        in_specs=x_specs + [
            pl.BlockSpec((None, 1, 6, d), lambda b, s, *_: (l, b, 0, 0)),
            pl.BlockSpec((None, 1, d), lay2),
            pl.BlockSpec((None, d, p_in), lay2),
            pl.BlockSpec((None, 1, d // 2), lay2),
            pl.BlockSpec((None, A_HEADS, A_CHUNK, A_CHUNK), lay3),
            pl.BlockSpec((None, A_HEADS, A_CHUNK, A_CHUNK), lay3),
            pl.BlockSpec((None, 2 * LANES, 2 * kw), lay2),
            pl.BlockSpec((None, 1, kw), lay2),
            pl.BlockSpec((None, 1, vw), lay2),
            pl.BlockSpec((None, d, d), lay2),
            pl.BlockSpec((None, 1, d), lay2),
            pl.BlockSpec((None, 2 * d, 2 * LANES), lay2),
            pl.BlockSpec((None, 1, LANES), lay2),
        ],
        out_specs=[
            pl.BlockSpec((ts, d), tok),
            pl.BlockSpec((ts * SUBLANES, LANES), tok),
            pl.BlockSpec((ts, LANES), tok),
            pl.BlockSpec((None, code_rows, LANES), lambda b, s, *_: (b * n_s + s, 0, 0)),
            pl.BlockSpec((SUBLANES, LANES), lambda b, s, *_: (0, 0)),
        ],
        scratch_shapes=[
            pltpu.VMEM((vw, kw), F32),
            pltpu.VMEM((ts, p_in), F32),
            pltpu.VMEM((ts, kw), F32),
            pltpu.VMEM((ts, vw), F32),
            pltpu.VMEM((SUBLANES, LANES), F32),
        ] + extra_scratch,
    )
    return pl.pallas_call(
        kern,
        grid_spec=grid_spec,
        out_shape=[
            jax.ShapeDtypeStruct((t, d), F32),
            jax.ShapeDtypeStruct((t * SUBLANES, LANES), F32),
            jax.ShapeDtypeStruct((t, LANES), F32),
            jax.ShapeDtypeStruct((t // ts, code_rows, LANES), I32),
            jax.ShapeDtypeStruct((SUBLANES, LANES), F32),
        ],
        compiler_params=pltpu.CompilerParams(
            dimension_semantics=("arbitrary", "arbitrary"), vmem_limit_bytes=VMEM_LIMIT),
        name="mixer",
    )(*prefetch, *x_args, mod, g1, win, avg, ws, bsb, wal, bal, og, wout, g2, wr, br)


def _slots_kernel(code_ref, cnt_ref, d_ref, be_ref, bv_ref, *, blk):
    cnt = cnt_ref[0:1, :]
    padded = jnp.floor((cnt + (blk - 1.0)) / blk) * blk
    r = lax.broadcasted_iota(I32, (LANES, LANES), 0)
    c = lax.broadcasted_iota(I32, (LANES, LANES), 1)
    upper = (r <= c).astype(F32)
    pad_end = jnp.dot(jnp.broadcast_to(padded, (SUBLANES, LANES)), upper, precision=HIGHEST,
                      preferred_element_type=F32)[0:1, :]
    pad_start = pad_end - padded

    code = code_ref[...].astype(F32)
    e = jnp.floor(code / CODE_SHIFT)
    slot = code - e * CODE_SHIFT
    for k in range(N_EXPERTS):
        slot = slot + jnp.where(e == k, pad_start[:, k:k + 1], 0.0)
    d_ref[...] = slot.astype(I32)

    shp = be_ref.shape
    jpos = (lax.broadcasted_iota(I32, shp, 0) * LANES + lax.broadcasted_iota(I32, shp, 1)).astype(F32) * blk
    be = jnp.zeros(shp, F32)
    for k in range(N_EXPERTS):
        be = be + jnp.where(pad_end[:, k:k + 1] <= jpos, 1.0, 0.0)
    be = jnp.minimum(be, N_EXPERTS - 1.0)
    seg_end = jnp.zeros(shp, F32)
    for k in range(N_EXPERTS):
        seg_end = seg_end + jnp.where(be == k, pad_start[:, k:k + 1] + cnt[:, k:k + 1], 0.0)
    be_ref[...] = be.astype(I32)
    bv_ref[...] = jnp.clip(seg_end - jpos, 0.0, float(blk)).astype(I32)


def _slots(codes2, counts, *, blk):
    kern = functools.partial(_slots_kernel, blk=blk)
    return pl.pallas_call(
        kern,
        out_shape=[
            jax.ShapeDtypeStruct(codes2.shape, I32),
            jax.ShapeDtypeStruct((SUBLANES, LANES), I32),
            jax.ShapeDtypeStruct((SUBLANES, LANES), I32),
        ],
        compiler_params=pltpu.CompilerParams(vmem_limit_bytes=VMEM_LIMIT),
        name="slots",
    )(codes2, counts)


def _dispatch_kernel(d_ref, bv_ref, h2_ref, xpad_ref, zero_scr, sem0, sem1, semz,
                     *, td, unroll, blk, n_blocks, blocks_per_step):
    i = pl.program_id(0)
    base = i * (2 * td)

    @pl.when(i == 0)
    def _():
        zero_scr[...] = jnp.zeros_like(zero_scr)

    def row_copy(r, dst, sem):
        return pltpu.make_async_copy(
            h2_ref.at[pl.ds(pl.multiple_of(r * SUBLANES, SUBLANES), SUBLANES)],
            xpad_ref.at[pl.ds(pl.multiple_of(dst * SUBLANES, SUBLANES), SUBLANES)], sem)

    def issue(jj, carry):
        for uu in range(unroll):
            r = jj * unroll + uu
            row_copy(r, d_ref[base + r], sem0).start(priority=0)
            row_copy(r, d_ref[base + td + r], sem1).start(priority=1)
        return carry

    lax.fori_loop(0, td // unroll, issue, 0)

    def pad_fill(act):
        for jb in range(blocks_per_step):
            j = i * blocks_per_step + jb
            jc = jnp.minimum(j, n_blocks - 1)
            valid = bv_ref[jc]
            n_pad = jnp.where(j < n_blocks, blk - valid, 0)
            row = jc * blk + valid
            size = blk
            while size >= 1:
                hit = (n_pad & size) != 0

                @pl.when(hit)
                def _(row=row, size=size):
                    cp = pltpu.make_async_copy(
                        zero_scr.at[pl.ds(0, size * SUBLANES)],
                        xpad_ref.at[pl.ds(pl.multiple_of(row * SUBLANES, SUBLANES), size * SUBLANES)],
                        semz)
                    act(cp)

                row = row + jnp.where(hit, size, 0)
                size //= 2

    pad_fill(lambda cp: cp.start())

    def drain(jj, carry):
        for uu in range(unroll):
            r = jj * unroll + uu
            row_copy(r, d_ref[base + r], sem0).wait()
            row_copy(r, d_ref[base + td + r], sem1).wait()
        return carry

    lax.fori_loop(0, td // unroll, drain, 0)
    pad_fill(lambda cp: cp.wait())


def _dispatch(dflat, block_valid, h2t, *, n_blocks, blk, td, unroll=8):
    t = h2t.shape[0] // SUBLANES
    n_steps = t // td
    blocks_per_step = (n_blocks + n_steps - 1) // n_steps
    kern = functools.partial(_dispatch_kernel, td=td, unroll=unroll, blk=blk, n_blocks=n_blocks,
                             blocks_per_step=blocks_per_step)
    return pl.pallas_call(
        kern,
        grid_spec=pltpu.PrefetchScalarGridSpec(
            num_scalar_prefetch=2,
            grid=(n_steps,),
            in_specs=[pl.BlockSpec((td * SUBLANES, LANES), lambda i, dd, bv: (i, 0))],
            out_specs=pl.BlockSpec(memory_space=pl.ANY),
            scratch_shapes=[
                pltpu.VMEM((blk * SUBLANES, LANES), F32),
                pltpu.SemaphoreType.DMA, pltpu.SemaphoreType.DMA, pltpu.SemaphoreType.DMA],
        ),
        out_shape=jax.ShapeDtypeStruct((n_blocks * blk * SUBLANES, LANES), F32),
        compiler_params=pltpu.CompilerParams(
            dimension_semantics=("arbitrary",), vmem_limit_bytes=VMEM_LIMIT),
        name="dispatch",
    )(dflat, block_valid, h2t)


def _experts_kernel(be_ref, bv_ref, x_ref, w1_ref, w3_ref, w2_ref, y_ref, w1b, w3b, w2b, *, blk):
    j = pl.program_id(0)
    valid = bv_ref[j]
    used = valid > 0
    prev = be_ref[jnp.maximum(j - 1, 0)]
    fresh = jnp.logical_or(j == 0, be_ref[j] != prev)

    @pl.when(jnp.logical_and(used, fresh))
    def _():
        w1b[...] = w1_ref[...].astype(BF16)
        w3b[...] = w3_ref[...].astype(BF16)
        w2b[...] = w2_ref[...].astype(BF16)

    @pl.when(used)
    def _():
        x = _tiles_to_rows(x_ref, blk).astype(BF16)
        a = jnp.dot(x, w1b[...], preferred_element_type=F32)
        g = jnp.dot(x, w3b[...], preferred_element_type=F32)
        hmid = ((a * jax.nn.sigmoid(a)) * g).astype(BF16)
        y = jnp.dot(hmid, w2b[...], preferred_element_type=F32)
        _rows_to_tiles(y_ref, y, blk)

    @pl.when(jnp.logical_not(used))
    def _():
        y_ref[...] = jnp.zeros_like(y_ref)


def _experts(l, block_expert, block_valid, xpad, w1, w3, w2, *, n_blocks, blk):
    ne, d, de = w1.shape[1:]
    w1 = w1.reshape(-1, d, de)
    w3 = w3.reshape(-1, d, de)
    w2 = w2.reshape(-1, de, d)
    wmap = lambda j, be, bv: (l * ne + be[j], 0, 0)
    kern = functools.partial(_experts_kernel, blk=blk)
    return pl.pallas_call(
        kern,
        grid_spec=pltpu.PrefetchScalarGridSpec(
            num_scalar_prefetch=2,
            grid=(n_blocks,),
            in_specs=[
                pl.BlockSpec((blk * SUBLANES, LANES), lambda j, be, bv: (j, 0)),
                pl.BlockSpec((None, d, de), wmap),
                pl.BlockSpec((None, d, de), wmap),
                pl.BlockSpec((None, de, d), wmap),
            ],
            out_specs=pl.BlockSpec((blk * SUBLANES, LANES), lambda j, be, bv: (j, 0)),
            scratch_shapes=[
                pltpu.VMEM((d, de), BF16),
                pltpu.VMEM((d, de), BF16),
                pltpu.VMEM((de, d), BF16),
            ],
        ),
        out_shape=jax.ShapeDtypeStruct(xpad.shape, F32),
        compiler_params=pltpu.CompilerParams(
            dimension_semantics=("arbitrary",), vmem_limit_bytes=VMEM_LIMIT),
        name="experts",
    )(block_expert, block_valid, xpad, w1, w3, w2)


def _combine_kernel(d_ref, x_ref, gate_ref, mod_ref, fg_ref, ypad_ref, o_ref,
                    y0buf, y1buf, sems, *, tc, unroll):
    def consume(y0, y1):
        gate = gate_ref[...]
        xo = x_ref[...] + mod_ref[0][5:6] * (y0 * gate[:, 0:1] + y1 * gate[:, 1:2])
        o_ref[...] = xo * lax.rsqrt(jnp.mean(xo * xo, axis=-1, keepdims=True) + EPS) * fg_ref[...]

    _gathered_rows_pipeline(pl.program_id(0), pl.num_programs(0), d_ref, ypad_ref, y0buf, y1buf, sems,
                            n=tc, unroll=unroll, consume=consume)


def _combine(l, dflat, x2, gate, mod, fgain, ypad, *, seq, tc, unroll=8):
    t, d = x2.shape
    per_b = seq // tc
    kern = functools.partial(_combine_kernel, tc=tc, unroll=unroll)
    return pl.pallas_call(
        kern,
        grid_spec=pltpu.PrefetchScalarGridSpec(
            num_scalar_prefetch=1,
            grid=(t // tc,),
            in_specs=[
                pl.BlockSpec((tc, d), lambda i, dd: (i, 0)),
                pl.BlockSpec((tc, LANES), lambda i, dd: (i, 0)),
                pl.BlockSpec((None, 1, 6, d), lambda i, dd: (l, i // per_b, 0, 0)),
                pl.BlockSpec((1, d), lambda i, dd: (0, 0)),
                pl.BlockSpec(memory_space=pl.ANY),
            ],
            out_specs=pl.BlockSpec((tc, d), lambda i, dd: (i, 0)),
            scratch_shapes=[
                pltpu.VMEM((2, tc * SUBLANES, LANES), F32),
                pltpu.VMEM((2, tc * SUBLANES, LANES), F32),
                pltpu.SemaphoreType.DMA((2, 2)),
            ],
        ),
        out_shape=jax.ShapeDtypeStruct((t, d), F32),
        compiler_params=pltpu.CompilerParams(
            dimension_semantics=("arbitrary",), vmem_limit_bytes=VMEM_LIMIT),
        name="combine",
    )(dflat, x2, gate, mod, fgain, ypad)


@jax.jit
def _forward(x, c, w_ada, b_ada, norm1_gain, norm2_gain, w_in, a_v_gain, a_w_s, a_b_s, b_w_alpha,
             b_b_alpha, b_out_gain, w_out, w_group, b_group, w_router, b_router, w1, w3, w2,
             final_gain):
    bsz, seq, d = x.shape
    depth = w_ada.shape[0]
    t = bsz * seq
    ts = min(TOKEN_TILE, seq)
    blk = MOE_BLOCK
    n_blocks = (2 * t + blk - 1) // blk + N_EXPERTS
    assert t <= CODE_SHIFT and n_blocks <= SUBLANES * LANES and seq % ts == 0 and ts % LANES == 0
    p_in = w_in.shape[-1]
    p_main = p_in - B_GATE_RANK
    kw = d // 4

    mod = _ada(c, w_ada, b_ada).reshape(depth, bsz, 6, d)

    win = jnp.concatenate(
        [w_in[..., :p_main],
         jnp.pad(w_in[..., p_main:], ((0, 0), (0, 0), (0, LANES - B_GATE_RANK)))], axis=-1).astype(BF16)
    wal = _stack_hi_lo(jnp.pad(b_w_alpha, ((0, 0), (0, LANES - B_GATE_RANK), (0, 0))))
    causal = jnp.tril(jnp.ones((A_CHUNK, A_CHUNK), dtype=bool))
    ws = jnp.where(causal[None, None], a_w_s, 0.0).astype(BF16)
    bsb = jnp.broadcast_to(a_b_s[..., None], a_b_s.shape + (d // 2 // A_HEADS,))
    wout = w_out.astype(BF16)
    n_r = N_GROUPS + N_EXPERTS
    wr = _stack_hi_lo(
        jnp.pad(jnp.concatenate([w_group, w_router], axis=-1), ((0, 0), (0, 0), (0, LANES - n_r))))
    br = jnp.pad(jnp.concatenate([b_group, b_router], axis=-1), ((0, 0), (0, LANES - n_r)))

    g1 = norm1_gain.reshape(depth, 1, d)
    g2 = norm2_gain.reshape(depth, 1, d)
    avg = a_v_gain.reshape(depth, 1, d // 2)
    bal = b_b_alpha.reshape(depth, 1, kw)
    og = b_out_gain.reshape(depth, 1, d // 2)
    br = br.reshape(depth, 1, LANES)
    fg = final_gain.reshape(1, d)

    x2 = x.reshape(t, d)
    moe_prev = None
    for l in range(depth):
        x2, h2t, gate, codes, counts = _mixer(
            l, x2, mod, g1, win, avg, ws, bsb, wal, bal, og, wout, g2, wr, br,
            bsz=bsz, seq=seq, ts=ts, moe_prev=moe_prev)
        slots, block_expert, block_valid = _slots(codes.reshape(-1, LANES), counts, blk=blk)
        dflat = slots.reshape(-1)
        block_valid = block_valid.reshape(-1)
        xpad = _dispatch(dflat, block_valid, h2t, n_blocks=n_blocks, blk=blk, td=ts)
        ypad = _experts(l, block_expert.reshape(-1), block_valid, xpad, w1, w3, w2,
                        n_blocks=n_blocks, blk=blk)
        moe_prev = (dflat, gate, ypad)
    x2 = _combine(depth - 1, dflat, x2, gate, mod, fg, ypad, seq=seq, tc=ts)
    return x2.reshape(bsz, seq, d)


def kernel(x, c, w_ada, b_ada, norm1_gain, norm2_gain, w_in, a_v_gain, a_w_s, a_b_s, b_w_alpha,
           b_b_alpha, b_out_gain, w_out, w_group, b_group, w_router, b_router, w1, w3, w2,
           final_gain):
    return _forward(x, c, w_ada, b_ada, norm1_gain, norm2_gain, w_in, a_v_gain, a_w_s, a_b_s,
                    b_w_alpha, b_b_alpha, b_out_gain, w_out, w_group, b_group, w_router, b_router,
                    w1, w3, w2, final_gain)
```

```python
import functools

import jax
import jax.numpy as jnp
from jax import lax
from jax.experimental import pallas as pl
from jax.experimental.pallas import tpu as pltpu

F32 = jnp.float32
BF16 = jnp.bfloat16
I32 = jnp.int32
HIGHEST = lax.Precision.HIGHEST

EPS = 1e-6
LANES = 128
SUBLANES = 8
A_HEADS = 4
A_CHUNK = 128
B_HEADS = 4
B_GATE_RANK = 16
B_GATE_TAU = 16.0
GLA_CHUNK = 128
N_GROUPS = 4
EXPERTS_PER_GROUP = 8
N_EXPERTS = N_GROUPS * EXPERTS_PER_GROUP
MOE_BLOCK = 512
TOKEN_TILE = 512
CODE_SHIFT = 65536.0
VMEM_LIMIT = 56 * 1024 * 1024


def _nt(a, b):
    return lax.dot_general(a, b, (((1,), (1,)), ((), ())), preferred_element_type=F32)


def _tn(a, b):
    return lax.dot_general(a, b, (((0,), (0,)), ((), ())), preferred_element_type=F32)


def _rows_to_tiles(ref, val, n_rows):
    for c in range(SUBLANES):
        ref[pl.ds(c, n_rows, stride=SUBLANES), :] = val[:, c * LANES:(c + 1) * LANES]


def _tiles_to_rows(ref, n_rows):
    return jnp.concatenate(
        [ref[pl.ds(c, n_rows, stride=SUBLANES), :] for c in range(SUBLANES)], axis=-1)


def _split_hi_lo(a):
    hi = a.astype(BF16)
    return hi, (a - hi.astype(F32)).astype(BF16)


def _stack_hi_lo(w):
    hi, lo = _split_hi_lo(w)
    return jnp.concatenate([jnp.concatenate([hi, lo], axis=-1),
                            jnp.concatenate([hi, jnp.zeros_like(lo)], axis=-1)], axis=-2)


def _col_to_lanes(col, n):
    r = lax.broadcasted_iota(I32, (LANES, LANES), 0)
    c = lax.broadcasted_iota(I32, (LANES, LANES), 1)
    eye = r == c
    rows = [jnp.sum(jnp.where(eye, col[g * LANES:(g + 1) * LANES, :], 0.0), axis=0, keepdims=True)
            for g in range(n // LANES)]
    return jnp.concatenate(rows, axis=0)


def _ada_kernel(c_ref, w_ref, b_ref, o_ref):
    c = c_ref[...]
    c_act = c * jax.nn.sigmoid(c)
    o_ref[0] = jnp.dot(c_act, w_ref[0], precision=HIGHEST, preferred_element_type=F32) + b_ref[0]


def _ada(c, w_ada, b_ada, tn=1024):
    depth, d, n = w_ada.shape
    bsz = c.shape[0]
    return pl.pallas_call(
        _ada_kernel,
        grid=(depth, n // tn),
        in_specs=[
            pl.BlockSpec((bsz, d), lambda l, j: (0, 0)),
            pl.BlockSpec((1, d, tn), lambda l, j: (l, 0, j)),
            pl.BlockSpec((1, 1, tn), lambda l, j: (l, 0, j)),
        ],
        out_specs=pl.BlockSpec((1, bsz, tn), lambda l, j: (l, 0, j)),
        out_shape=jax.ShapeDtypeStruct((depth, bsz, n), F32),
        compiler_params=pltpu.CompilerParams(
            dimension_semantics=("arbitrary", "arbitrary"), vmem_limit_bytes=VMEM_LIMIT),
        name="ada_mod",
    )(c, w_ada, b_ada.reshape(depth, 1, n))


def _gather_rows_start(d_ref, base, ypad_ref, buf0, buf1, sem0, sem1, *, n, unroll):
    def row_copy(r, src, buf, sem):
        return pltpu.make_async_copy(
            ypad_ref.at[pl.ds(pl.multiple_of(src * SUBLANES, SUBLANES), SUBLANES)],
            buf.at[pl.ds(pl.multiple_of(r * SUBLANES, SUBLANES), SUBLANES)], sem)

    def issue(jj, carry):
        for uu in range(unroll):
            r = jj * unroll + uu
            row_copy(r, d_ref[base + r], buf0, sem0).start(priority=0)
            row_copy(r, d_ref[base + n + r], buf1, sem1).start(priority=1)
        return carry

    if unroll == n:
        issue(0, 0)
    else:
        lax.fori_loop(0, n // unroll, issue, 0)


def _gather_rows_wait(ypad_ref, buf0, buf1, sem0, sem1, *, n):
    pltpu.make_async_copy(ypad_ref.at[pl.ds(0, n * SUBLANES)], buf0, sem0).wait()
    pltpu.make_async_copy(ypad_ref.at[pl.ds(0, n * SUBLANES)], buf1, sem1).wait()


def _gathered_rows_pipeline(i, n_tiles, d_ref, ypad_ref, y0buf, y1buf, sems, *, n, unroll, consume):
    start = functools.partial(_gather_rows_start, d_ref, n=n, unroll=unroll)

    @pl.when(i == 0)
    def _():
        start(0, ypad_ref, y0buf.at[0], y1buf.at[0], sems.at[0, 0], sems.at[0, 1])

    for p in range(2):
        @pl.when(i % 2 == p)
        def _(p=p):
            _gather_rows_wait(ypad_ref, y0buf.at[p], y1buf.at[p], sems.at[p, 0], sems.at[p, 1], n=n)

            @pl.when(i + 1 < n_tiles)
            def _():
                q = 1 - p
                start((i + 1) * (2 * n), ypad_ref, y0buf.at[q], y1buf.at[q],
                      sems.at[q, 0], sems.at[q, 1])

            consume(_tiles_to_rows(y0buf.at[p], n), _tiles_to_rows(y1buf.at[p], n))


def _mixer_kernel(*refs, ts, d, fused, unroll):
    if fused:
        d_ref, x_ref, gatep_ref, modp_ref, ypad_ref = refs[:5]
        refs = refs[5:]
        xin_scr, y0buf, y1buf, sem0, sem1 = refs[-5:]
        refs = refs[:-5]
    else:
        x_ref = refs[0]
        refs = refs[1:]
    (mod_ref, g1_ref, win_ref, avg_ref, ws_ref, bsb_ref, wal_ref, bal_ref, og_ref, wout_ref, g2_ref,
     wr_ref, br_ref, xo_ref, h2_ref, gate_ref, code_ref, cnt_ref,
     st_scr, proj_scr, la_scr, o_scr, cnt_scr) = refs
    a_w = d // 2
    kw = d // 4
    vw = d // 2
    hk = kw // B_HEADS
    hv = vw // B_HEADS
    o_q, o_k, o_v, o_r, o_a = 2 * a_w, 2 * a_w + kw, 2 * a_w + 2 * kw, 2 * a_w + 2 * kw + vw, \
        2 * a_w + 2 * kw + 2 * vw

    b = pl.program_id(0)
    s = pl.program_id(1)

    @pl.when(s == 0)
    def _():
        st_scr[...] = jnp.zeros_like(st_scr)

    @pl.when(jnp.logical_and(b == 0, s == 0))
    def _():
        cnt_scr[...] = jnp.zeros_like(cnt_scr)

    if fused:
        i = b * pl.num_programs(1) + s
        n_tiles = pl.num_programs(0) * pl.num_programs(1)

        @pl.when(i == 0)
        def _():
            _gather_rows_start(d_ref, 0, ypad_ref, y0buf.at[0], y1buf.at[0], sem0.at[0], sem1.at[0],
                               n=ts, unroll=unroll)

        for p in range(2):
            @pl.when(i % 2 == p)
            def _(p=p):
                _gather_rows_wait(ypad_ref, y0buf.at[p], y1buf.at[p], sem0.at[p], sem1.at[p], n=ts)
                gate_p = gatep_ref[...]
                moe = _tiles_to_rows(y0buf.at[p], ts) * gate_p[:, 0:1] \
                    + _tiles_to_rows(y1buf.at[p], ts) * gate_p[:, 1:2]
                xin_scr[...] = x_ref[...] + modp_ref[0][5:6] * moe

        x = xin_scr[...]
        nslot = (i + 1) % 2
        _gather_rows_start(d_ref, jnp.minimum(i + 1, n_tiles - 1) * (2 * ts), ypad_ref,
                           y0buf.at[nslot], y1buf.at[nslot], sem0.at[nslot], sem1.at[nslot],
                           n=ts, unroll=ts)
    else:
        x = x_ref[...]
    mod = mod_ref[0]
    sh1, sc1, g1 = mod[0:1], mod[1:2], mod[2:3]
    sh2, sc2 = mod[3:4], mod[4:5]

    h = x * lax.rsqrt(jnp.mean(x * x, axis=-1, keepdims=True) + EPS) * g1_ref[...]
    h = h * (1.0 + sc1) + sh1
    proj_scr[...] = jnp.dot(h.astype(BF16), win_ref[...], preferred_element_type=F32)

    u = jax.nn.gelu(proj_scr[:, 0:a_w])
    v = jax.nn.gelu(proj_scr[:, a_w:2 * a_w])
    vn = v * lax.rsqrt(jnp.mean(v * v, axis=-1, keepdims=True) + EPS) * avg_ref[...]
    vn = vn.astype(BF16)
    ahd = a_w // A_HEADS
    z_rows = []
    for ci in range(ts // A_CHUNK):
        z_heads = []
        for hd in range(A_HEADS):
            vc = vn[ci * A_CHUNK:(ci + 1) * A_CHUNK, hd * ahd:(hd + 1) * ahd]
            z_heads.append(jnp.dot(ws_ref[hd], vc, preferred_element_type=F32) + bsb_ref[hd])
        z_rows.append(jnp.concatenate(z_heads, axis=-1))
    y_a = u * jnp.concatenate(z_rows, axis=0)

    a_hi, a_lo = _split_hi_lo(proj_scr[:, o_a:o_a + LANES])
    xg2 = jnp.dot(jnp.concatenate([a_hi, a_lo], axis=-1), wal_ref[...], preferred_element_type=F32)
    xg = xg2[:, :kw] + xg2[:, kw:] + bal_ref[...]
    la_scr[...] = (jnp.minimum(xg, 0.0) - jnp.log(1.0 + jnp.exp(-jnp.abs(xg)))) / B_GATE_TAU

    c = GLA_CHUNK
    row = lax.broadcasted_iota(I32, (c, c), 0)
    col = lax.broadcasted_iota(I32, (c, c), 1)
    causal = col <= row
    tri = causal.astype(BF16)
    klane = lax.broadcasted_iota(I32, (1, kw), 1)
    bd_r = lax.broadcasted_iota(I32, (vw, kw), 0) // hv
    bd_c = lax.broadcasted_iota(I32, (vw, kw), 1) // hk
    bd_mask = (bd_r == bd_c).astype(F32)
    mid = c // 2 - 1

    st = st_scr[...]
    for j in range(ts // c):
        r0 = j * c
        g = la_scr[r0:r0 + c, :]
        q = proj_scr[r0:r0 + c, o_q:o_q + kw] * (hk ** -0.5)
        k = proj_scr[r0:r0 + c, o_k:o_k + kw]
        vv = proj_scr[r0:r0 + c, o_v:o_v + vw].astype(BF16)
        gp1 = g.astype(BF16)
        gp2, gp3 = _split_hi_lo(g - gp1.astype(F32))
        b3 = jnp.dot(tri, jnp.concatenate([gp1, gp2, gp3], axis=-1), preferred_element_type=F32)
        bcum = b3[:, :kw] + b3[:, kw:2 * kw] + b3[:, 2 * kw:]
        blast = bcum[c - 1:c, :]
        bmid = bcum[mid:mid + 1, :]
        qe = (q * jnp.exp(bcum)).astype(BF16)
        qt = q * jnp.exp(bcum - bmid)
        kt = (k * jnp.exp(bmid - bcum)).astype(BF16)
        kd = (k * jnp.exp(blast - bcum)).astype(BF16)
        o_inter = _nt(qe, st.astype(BF16))
        o_heads = []
        for hd in range(B_HEADS):
            m_h = jnp.logical_and(klane >= hd * hk, klane < (hd + 1) * hk)
            a_h = _nt(jnp.where(m_h, qt, 0.0).astype(BF16), kt)
            a_h = jnp.where(causal, a_h, 0.0).astype(BF16)
            o_heads.append(jnp.dot(a_h, vv[:, hd * hv:(hd + 1) * hv], preferred_element_type=F32))
        o_scr[r0:r0 + c, :] = o_inter + jnp.concatenate(o_heads, axis=-1)
        st = st * jnp.exp(blast) + _tn(vv, kd) * bd_mask
    st_scr[...] = st

    o = o_scr[...]
    og = og_ref[...]
    yb_heads = []
    for hd in range(B_HEADS):
        oh = o[:, hd * hv:(hd + 1) * hv]
        oh = oh * lax.rsqrt(jnp.mean(oh * oh, axis=-1, keepdims=True) + EPS)
        yb_heads.append(oh * og[:, hd * hv:(hd + 1) * hv])
    r = proj_scr[:, o_r:o_r + vw]
    y_b = jnp.concatenate(yb_heads, axis=-1) * (r * jax.nn.sigmoid(r))

    mixed = jnp.dot(jnp.concatenate([y_a, y_b], axis=-1).astype(BF16), wout_ref[...],
                    preferred_element_type=F32)
    xn = x + g1 * mixed
    xo_ref[...] = xn

    h2 = xn * lax.rsqrt(jnp.mean(xn * xn, axis=-1, keepdims=True) + EPS) * g2_ref[...]
    h2 = h2 * (1.0 + sc2) + sh2
    _rows_to_tiles(h2_ref, h2, ts)

    h2h, h2l = _split_hi_lo(h2)
    lg2 = jnp.dot(jnp.concatenate([h2h, h2l], axis=-1), wr_ref[...], preferred_element_type=F32)
    logits = lg2[:, :LANES] + lg2[:, LANES:] + br_ref[...]
    lane = lax.broadcasted_iota(I32, (ts, LANES), 1).astype(F32)
    neg = -jnp.inf
    big = float(LANES)
    is_g = lane < N_GROUPS
    gl = jnp.where(is_g, logits, neg)
    gmax = jnp.max(gl, axis=-1, keepdims=True)
    grp = jnp.min(jnp.where(gl == gmax, lane, big), axis=-1, keepdims=True)
    gsum = jnp.sum(jnp.where(is_g, jnp.exp(logits - gmax), 0.0), axis=-1, keepdims=True)
    p_grp = 1.0 / gsum
    lo = N_GROUPS + EXPERTS_PER_GROUP * grp
    el = jnp.where(jnp.logical_and(lane >= lo, lane < lo + EXPERTS_PER_GROUP), logits, neg)
    t1 = jnp.max(el, axis=-1, keepdims=True)
    i1 = jnp.min(jnp.where(el == t1, lane, big), axis=-1, keepdims=True)
    el2 = jnp.where(lane == i1, neg, el)
    t2 = jnp.max(el2, axis=-1, keepdims=True)
    i2 = jnp.min(jnp.where(el2 == t2, lane, big), axis=-1, keepdims=True)
    e2 = jnp.exp(t2 - t1)
    w0 = (1.0 / (1.0 + e2)) * p_grp
    w1 = (e2 / (1.0 + e2)) * p_grp
    e0 = i1 - N_GROUPS
    e1 = i2 - N_GROUPS

    sel0 = lane == e0
    sel1 = lane == e1
    onehot = jnp.logical_or(sel0, sel1)
    rr = lax.broadcasted_iota(I32, (ts, ts), 0)
    cc = lax.broadcasted_iota(I32, (ts, ts), 1)
    before = (cc < rr).astype(BF16)
    prefix = jnp.dot(before, onehot.astype(BF16), preferred_element_type=F32) + cnt_scr[0:1, :]
    rank0 = jnp.sum(jnp.where(sel0, prefix, 0.0), axis=-1, keepdims=True)
    rank1 = jnp.sum(jnp.where(sel1, prefix, 0.0), axis=-1, keepdims=True)
    cnt_new = cnt_scr[0:1, :] + jnp.sum(onehot.astype(F32), axis=0, keepdims=True)
    cnt_scr[...] = jnp.broadcast_to(cnt_new, cnt_scr.shape)
    cnt_ref[...] = jnp.broadcast_to(cnt_new, cnt_ref.shape)

    gate_ref[...] = jnp.where(lane == 0, w0, jnp.where(lane == 1, w1, 0.0))
    code_ref[...] = jnp.concatenate(
        [_col_to_lanes(e0 * CODE_SHIFT + rank0, ts), _col_to_lanes(e1 * CODE_SHIFT + rank1, ts)],
        axis=0).astype(I32)

    if fused:
        @pl.when(i == n_tiles - 1)
        def _():
            _gather_rows_wait(ypad_ref, y0buf.at[nslot], y1buf.at[nslot], sem0.at[nslot],
                              sem1.at[nslot], n=ts)


def _mixer(l, x2, mod, g1, win, avg, ws, bsb, wal, bal, og, wout, g2, wr, br, *, bsz, seq, ts,
           moe_prev=None, unroll=8):
    t, d = x2.shape
    n_s = seq // ts
    p_in = win.shape[-1]
    kw, vw = d // 4, d // 2
    lay2 = lambda b, s, *_: (l, 0, 0)
    lay3 = lambda b, s, *_: (l, 0, 0, 0)
    tok = lambda b, s, *_: (b * n_s + s, 0)
    code_rows = 2 * ts // LANES
    fused = moe_prev is not None
    kern = functools.partial(_mixer_kernel, ts=ts, d=d, fused=fused, unroll=unroll)
    x_specs = [pl.BlockSpec((ts, d), tok)]
    x_args = [x2]
    extra_scratch = []
    prefetch = []
    if fused:
        dflat, gate_prev, ypad_prev = moe_prev
        prefetch = [dflat]
        x_specs += [
            pl.BlockSpec((ts, LANES), tok),
            pl.BlockSpec((None, 1, 6, d), lambda b, s, *_: (l - 1, b, 0, 0)),
            pl.BlockSpec(memory_space=pl.ANY),
        ]
        x_args += [gate_prev, mod, ypad_prev]
        extra_scratch = [
            pltpu.VMEM((ts, d), F32),
            pltpu.VMEM((2, ts * SUBLANES, LANES), F32),
            pltpu.VMEM((2, ts * SUBLANES, LANES), F32),
            pltpu.SemaphoreType.DMA((2,)),
            pltpu.SemaphoreType.DMA((2,)),
        ]
    grid_spec = pltpu.PrefetchScalarGridSpec(
        num_scalar_prefetch=len(prefetch),
        grid=(bsz, n_s),
        in_specs=x_specs + [
            pl.BlockSpec((None, 1, 6, d), lambda b, s, *_: (l, b, 0, 0)),
            pl.BlockSpec((None, 1, d), lay2),
            pl.BlockSpec((None, d, p_in), lay2),
            pl.BlockSpec((None, 1, d // 2), lay2),
            pl.BlockSpec((None, A_HEADS, A_CHUNK, A_CHUNK), lay3),
            pl.BlockSpec((None, A_HEADS, A_CHUNK, A_CHUNK), lay3),
            pl.BlockSpec((None, 2 * LANES, 2 * kw), lay2),
            pl.BlockSpec((None, 1, kw), lay2),
            pl.BlockSpec((None, 1, vw), lay2),
            pl.BlockSpec((None, d, d), lay2),
            pl.BlockSpec((None, 1, d), lay2),
            pl.BlockSpec((None, 2 * d, 2 * LANES), lay2),
            pl.BlockSpec((None, 1, LANES), lay2),
        ],
        out_specs=[
            pl.BlockSpec((ts, d), tok),
            pl.BlockSpec((ts * SUBLANES, LANES), tok),
            pl.BlockSpec((ts, LANES), tok),
            pl.BlockSpec((None, code_rows, LANES), lambda b, s, *_: (b * n_s + s, 0, 0)),
            pl.BlockSpec((SUBLANES, LANES), lambda b, s, *_: (0, 0)),
        ],
        scratch_shapes=[
            pltpu.VMEM((vw, kw), F32),
            pltpu.VMEM((ts, p_in), F32),
            pltpu.VMEM((ts, kw), F32),
            pltpu.VMEM((ts, vw), F32),
            pltpu.VMEM((SUBLANES, LANES), F32),
        ] + extra_scratch,
    )
    return pl.pallas_call(
        kern,
        grid_spec=grid_spec,
        out_shape=[
            jax.ShapeDtypeStruct((t, d), F32),
            jax.ShapeDtypeStruct((t * SUBLANES, LANES), F32),
            jax.ShapeDtypeStruct((t, LANES), F32),
            jax.ShapeDtypeStruct((t // ts, code_rows, LANES), I32),
            jax.ShapeDtypeStruct((SUBLANES, LANES), F32),
        ],
        compiler_params=pltpu.CompilerParams(
            dimension_semantics=("arbitrary", "arbitrary"), vmem_limit_bytes=VMEM_LIMIT),
        name="mixer",
    )(*prefetch, *x_args, mod, g1, win, avg, ws, bsb, wal, bal, og, wout, g2, wr, br)


def _slots_kernel(code_ref, cnt_ref, d_ref, be_ref, bv_ref, *, blk):
    cnt = cnt_ref[0:1, :]
    padded = jnp.floor((cnt + (blk - 1.0)) / blk) * blk
    r = lax.broadcasted_iota(I32, (LANES, LANES), 0)
    c = lax.broadcasted_iota(I32, (LANES, LANES), 1)
    upper = (r <= c).astype(F32)
    pad_end = jnp.dot(jnp.broadcast_to(padded, (SUBLANES, LANES)), upper, precision=HIGHEST,
                      preferred_element_type=F32)[0:1, :]
    pad_start = pad_end - padded

    code = code_ref[...].astype(F32)
    e = jnp.floor(code / CODE_SHIFT)
    slot = code - e * CODE_SHIFT
    for k in range(N_EXPERTS):
        slot = slot + jnp.where(e == k, pad_start[:, k:k + 1], 0.0)
    d_ref[...] = slot.astype(I32)

    shp = be_ref.shape
    jpos = (lax.broadcasted_iota(I32, shp, 0) * LANES + lax.broadcasted_iota(I32, shp, 1)).astype(F32) * blk
    be = jnp.zeros(shp, F32)
    for k in range(N_EXPERTS):
        be = be + jnp.where(pad_end[:, k:k + 1] <= jpos, 1.0, 0.0)
    be = jnp.minimum(be, N_EXPERTS - 1.0)
    seg_end = jnp.zeros(shp, F32)
    for k in range(N_EXPERTS):
        seg_end = seg_end + jnp.where(be == k, pad_start[:, k:k + 1] + cnt[:, k:k + 1], 0.0)
    be_ref[...] = be.astype(I32)
    bv_ref[...] = jnp.clip(seg_end - jpos, 0.0, float(blk)).astype(I32)


def _slots(codes2, counts, *, blk):
    kern = functools.partial(_slots_kernel, blk=blk)
    return pl.pallas_call(
        kern,
        out_shape=[
            jax.ShapeDtypeStruct(codes2.shape, I32),
            jax.ShapeDtypeStruct((SUBLANES, LANES), I32),
            jax.ShapeDtypeStruct((SUBLANES, LANES), I32),
        ],
        compiler_params=pltpu.CompilerParams(vmem_limit_bytes=VMEM_LIMIT),
        name="slots",
    )(codes2, counts)


def _dispatch_kernel(d_ref, bv_ref, h2_ref, xpad_ref, zero_scr, sem0, sem1, semz,
                     *, td, unroll, blk, n_blocks, blocks_per_step):
    i = pl.program_id(0)
    base = i * (2 * td)

    @pl.when(i == 0)
    def _():
        zero_scr[...] = jnp.zeros_like(zero_scr)

    def row_copy(r, dst, sem):
        return pltpu.make_async_copy(
            h2_ref.at[pl.ds(pl.multiple_of(r * SUBLANES, SUBLANES), SUBLANES)],
            xpad_ref.at[pl.ds(pl.multiple_of(dst * SUBLANES, SUBLANES), SUBLANES)], sem)

    def issue(jj, carry):
        for uu in range(unroll):
            r = jj * unroll + uu
            row_copy(r, d_ref[base + r], sem0).start(priority=0)
            row_copy(r, d_ref[base + td + r], sem1).start(priority=1)
        return carry

    lax.fori_loop(0, td // unroll, issue, 0)

    def pad_fill(act):
        for jb in range(blocks_per_step):
            j = i * blocks_per_step + jb
            jc = jnp.minimum(j, n_blocks - 1)
            valid = bv_ref[jc]
            n_pad = jnp.where(j < n_blocks, blk - valid, 0)
            row = jc * blk + valid
            size = blk
            while size >= 1:
                hit = (n_pad & size) != 0

                @pl.when(hit)
                def _(row=row, size=size):
                    cp = pltpu.make_async_copy(
                        zero_scr.at[pl.ds(0, size * SUBLANES)],
                        xpad_ref.at[pl.ds(pl.multiple_of(row * SUBLANES, SUBLANES), size * SUBLANES)],
                        semz)
                    act(cp)

                row = row + jnp.where(hit, size, 0)
                size //= 2

    pad_fill(lambda cp: cp.start())

    for sem in (sem0, sem1):
        pltpu.make_async_copy(h2_ref, xpad_ref.at[pl.ds(0, td * SUBLANES)], sem).wait()
    pad_fill(lambda cp: cp.wait())


def _dispatch(dflat, block_valid, h2t, *, n_blocks, blk, td, unroll=16):
    t = h2t.shape[0] // SUBLANES
    n_steps = t // td
    blocks_per_step = (n_blocks + n_steps - 1) // n_steps
    kern = functools.partial(_dispatch_kernel, td=td, unroll=unroll, blk=blk, n_blocks=n_blocks,
                             blocks_per_step=blocks_per_step)
    return pl.pallas_call(
        kern,
        grid_spec=pltpu.PrefetchScalarGridSpec(
            num_scalar_prefetch=2,
            grid=(n_steps,),
            in_specs=[pl.BlockSpec((td * SUBLANES, LANES), lambda i, dd, bv: (i, 0))],
            out_specs=pl.BlockSpec(memory_space=pl.ANY),
            scratch_shapes=[
                pltpu.VMEM((blk * SUBLANES, LANES), F32),
                pltpu.SemaphoreType.DMA, pltpu.SemaphoreType.DMA, pltpu.SemaphoreType.DMA],
        ),
        out_shape=jax.ShapeDtypeStruct((n_blocks * blk * SUBLANES, LANES), F32),
        compiler_params=pltpu.CompilerParams(
            dimension_semantics=("arbitrary",), vmem_limit_bytes=VMEM_LIMIT),
        name="dispatch",
    )(dflat, block_valid, h2t)


def _experts_kernel(be_ref, bv_ref, x_ref, w1_ref, w3_ref, w2_ref, y_ref, w1b, w3b, w2b, *, blk):
    j = pl.program_id(0)
    valid = bv_ref[j]
    used = valid > 0
    prev = be_ref[jnp.maximum(j - 1, 0)]
    fresh = jnp.logical_or(j == 0, be_ref[j] != prev)

    @pl.when(jnp.logical_and(used, fresh))
    def _():
        w1b[...] = w1_ref[...].astype(BF16)
        w3b[...] = w3_ref[...].astype(BF16)
        w2b[...] = w2_ref[...].astype(BF16)

    @pl.when(used)
    def _():
        x = _tiles_to_rows(x_ref, blk).astype(BF16)
        a = jnp.dot(x, w1b[...], preferred_element_type=F32)
        g = jnp.dot(x, w3b[...], preferred_element_type=F32)
        hmid = ((a * jax.nn.sigmoid(a)) * g).astype(BF16)
        y = jnp.dot(hmid, w2b[...], preferred_element_type=F32)
        _rows_to_tiles(y_ref, y, blk)

    @pl.when(jnp.logical_not(used))
    def _():
        y_ref[...] = jnp.zeros_like(y_ref)


def _experts(l, block_expert, block_valid, xpad, w1, w3, w2, *, n_blocks, blk):
    ne, d, de = w1.shape[1:]
    w1 = w1.reshape(-1, d, de)
    w3 = w3.reshape(-1, d, de)
    w2 = w2.reshape(-1, de, d)
    wmap = lambda j, be, bv: (l * ne + be[j], 0, 0)
    kern = functools.partial(_experts_kernel, blk=blk)
    return pl.pallas_call(
        kern,
        grid_spec=pltpu.PrefetchScalarGridSpec(
            num_scalar_prefetch=2,
            grid=(n_blocks,),
            in_specs=[
                pl.BlockSpec((blk * SUBLANES, LANES), lambda j, be, bv: (j, 0)),
                pl.BlockSpec((None, d, de), wmap),
                pl.BlockSpec((None, d, de), wmap),
                pl.BlockSpec((None, de, d), wmap),
            ],
            out_specs=pl.BlockSpec((blk * SUBLANES, LANES), lambda j, be, bv: (j, 0)),
            scratch_shapes=[
                pltpu.VMEM((d, de), BF16),
                pltpu.VMEM((d, de), BF16),
                pltpu.VMEM((de, d), BF16),
            ],
        ),
        out_shape=jax.ShapeDtypeStruct(xpad.shape, F32),
        compiler_params=pltpu.CompilerParams(
            dimension_semantics=("arbitrary",), vmem_limit_bytes=VMEM_LIMIT),
        name="experts",
    )(block_expert, block_valid, xpad, w1, w3, w2)


def _combine_kernel(d_ref, x_ref, gate_ref, mod_ref, fg_ref, ypad_ref, o_ref,
                    y0buf, y1buf, sems, *, tc, unroll):
    def consume(y0, y1):
        gate = gate_ref[...]
        xo = x_ref[...] + mod_ref[0][5:6] * (y0 * gate[:, 0:1] + y1 * gate[:, 1:2])
        o_ref[...] = xo * lax.rsqrt(jnp.mean(xo * xo, axis=-1, keepdims=True) + EPS) * fg_ref[...]

    _gathered_rows_pipeline(pl.program_id(0), pl.num_programs(0), d_ref, ypad_ref, y0buf, y1buf, sems,
                            n=tc, unroll=unroll, consume=consume)


def _combine(l, dflat, x2, gate, mod, fgain, ypad, *, seq, tc, unroll=8):
    t, d = x2.shape
    per_b = seq // tc
    kern = functools.partial(_combine_kernel, tc=tc, unroll=unroll)
    return pl.pallas_call(
        kern,
        grid_spec=pltpu.PrefetchScalarGridSpec(
            num_scalar_prefetch=1,
            grid=(t // tc,),
            in_specs=[
                pl.BlockSpec((tc, d), lambda i, dd: (i, 0)),
                pl.BlockSpec((tc, LANES), lambda i, dd: (i, 0)),
                pl.BlockSpec((None, 1, 6, d), lambda i, dd: (l, i // per_b, 0, 0)),
                pl.BlockSpec((1, d), lambda i, dd: (0, 0)),
                pl.BlockSpec(memory_space=pl.ANY),
            ],
            out_specs=pl.BlockSpec((tc, d), lambda i, dd: (i, 0)),
            scratch_shapes=[
                pltpu.VMEM((2, tc * SUBLANES, LANES), F32),
                pltpu.VMEM((2, tc * SUBLANES, LANES), F32),
                pltpu.SemaphoreType.DMA((2, 2)),
            ],
        ),
        out_shape=jax.ShapeDtypeStruct((t, d), F32),
        compiler_params=pltpu.CompilerParams(
            dimension_semantics=("arbitrary",), vmem_limit_bytes=VMEM_LIMIT),
        name="combine",
    )(dflat, x2, gate, mod, fgain, ypad)


@jax.jit
def _forward(x, c, w_ada, b_ada, norm1_gain, norm2_gain, w_in, a_v_gain, a_w_s, a_b_s, b_w_alpha,
             b_b_alpha, b_out_gain, w_out, w_group, b_group, w_router, b_router, w1, w3, w2,
             final_gain):
    bsz, seq, d = x.shape
    depth = w_ada.shape[0]
    t = bsz * seq
    ts = min(TOKEN_TILE, seq)
    blk = MOE_BLOCK
    n_blocks = (2 * t + blk - 1) // blk + N_EXPERTS
    assert t <= CODE_SHIFT and n_blocks <= SUBLANES * LANES and seq % ts == 0 and ts % LANES == 0
    p_in = w_in.shape[-1]
    p_main = p_in - B_GATE_RANK
    kw = d // 4

    mod = _ada(c, w_ada, b_ada).reshape(depth, bsz, 6, d)

    win = jnp.concatenate(
        [w_in[..., :p_main],
         jnp.pad(w_in[..., p_main:], ((0, 0), (0, 0), (0, LANES - B_GATE_RANK)))], axis=-1).astype(BF16)
    wal = _stack_hi_lo(jnp.pad(b_w_alpha, ((0, 0), (0, LANES - B_GATE_RANK), (0, 0))))
    causal = jnp.tril(jnp.ones((A_CHUNK, A_CHUNK), dtype=bool))
    ws = jnp.where(causal[None, None], a_w_s, 0.0).astype(BF16)
    bsb = jnp.broadcast_to(a_b_s[..., None], a_b_s.shape + (d // 2 // A_HEADS,))
    wout = w_out.astype(BF16)
    n_r = N_GROUPS + N_EXPERTS
    wr = _stack_hi_lo(
        jnp.pad(jnp.concatenate([w_group, w_router], axis=-1), ((0, 0), (0, 0), (0, LANES - n_r))))
    br = jnp.pad(jnp.concatenate([b_group, b_router], axis=-1), ((0, 0), (0, LANES - n_r)))

    g1 = norm1_gain.reshape(depth, 1, d)
    g2 = norm2_gain.reshape(depth, 1, d)
    avg = a_v_gain.reshape(depth, 1, d // 2)
    bal = b_b_alpha.reshape(depth, 1, kw)
    og = b_out_gain.reshape(depth, 1, d // 2)
    br = br.reshape(depth, 1, LANES)
    fg = final_gain.reshape(1, d)

    x2 = x.reshape(t, d)
    moe_prev = None
    for l in range(depth):
        x2, h2t, gate, codes, counts = _mixer(
            l, x2, mod, g1, win, avg, ws, bsb, wal, bal, og, wout, g2, wr, br,
            bsz=bsz, seq=seq, ts=ts, moe_prev=moe_prev)
        slots, block_expert, block_valid = _slots(codes.reshape(-1, LANES), counts, blk=blk)
        dflat = slots.reshape(-1)
        block_valid = block_valid.reshape(-1)
        xpad = _dispatch(dflat, block_valid, h2t, n_blocks=n_blocks, blk=blk, td=ts)
        ypad = _experts(l, block_expert.reshape(-1), block_valid, xpad, w1, w3, w2,
                        n_blocks=n_blocks, blk=blk)
        moe_prev = (dflat, gate, ypad)
    x2 = _combine(depth - 1, dflat, x2, gate, mod, fg, ypad, seq=seq, tc=ts)
    return x2.reshape(bsz, seq, d)


def kernel(x, c, w_ada, b_ada, norm1_gain, norm2_gain, w_in, a_v_gain, a_w_s, a_b_s, b_w_alpha,
           b_b_alpha, b_out_gain, w_out, w_group, b_group, w_router, b_router, w1, w3, w2,
           final_gain):
    return _forward(x, c, w_ada, b_ada, norm1_gain, norm2_gain, w_in, a_v_gain, a_w_s, a_b_s,
                    b_w_alpha, b_b_alpha, b_out_gain, w_out, w_group, b_group, w_router, b_router,
                    w1, w3, w2, final_gain)
```

```python
import functools

import jax
import jax.numpy as jnp
from jax import lax
from jax.experimental import pallas as pl
from jax.experimental.pallas import tpu as pltpu

F32 = jnp.float32
BF16 = jnp.bfloat16
I32 = jnp.int32
HIGHEST = lax.Precision.HIGHEST

EPS = 1e-6
LANES = 128
SUBLANES = 8
A_HEADS = 4
A_CHUNK = 128
B_HEADS = 4
B_GATE_RANK = 16
B_GATE_TAU = 16.0
GLA_CHUNK = 128
N_GROUPS = 4
EXPERTS_PER_GROUP = 8
N_EXPERTS = N_GROUPS * EXPERTS_PER_GROUP
MOE_BLOCK = 512
TOKEN_TILE = 512
CODE_SHIFT = 65536.0
VMEM_LIMIT = 56 * 1024 * 1024


def _nt(a, b):
    return lax.dot_general(a, b, (((1,), (1,)), ((), ())), preferred_element_type=F32)


def _tn(a, b):
    return lax.dot_general(a, b, (((0,), (0,)), ((), ())), preferred_element_type=F32)


def _rows_to_tiles(ref, val, n_rows):
    for c in range(SUBLANES):
        ref[pl.ds(c, n_rows, stride=SUBLANES), :] = val[:, c * LANES:(c + 1) * LANES]


def _tiles_to_rows(ref, n_rows):
    return jnp.concatenate(
        [ref[pl.ds(c, n_rows, stride=SUBLANES), :] for c in range(SUBLANES)], axis=-1)


def _split_hi_lo(a):
    hi = a.astype(BF16)
    return hi, (a - hi.astype(F32)).astype(BF16)


def _stack_hi_lo(w):
    hi, lo = _split_hi_lo(w)
    return jnp.concatenate([jnp.concatenate([hi, lo], axis=-1),
                            jnp.concatenate([hi, jnp.zeros_like(lo)], axis=-1)], axis=-2)


def _col_to_lanes(col, n):
    r = lax.broadcasted_iota(I32, (LANES, LANES), 0)
    c = lax.broadcasted_iota(I32, (LANES, LANES), 1)
    eye = r == c
    rows = [jnp.sum(jnp.where(eye, col[g * LANES:(g + 1) * LANES, :], 0.0), axis=0, keepdims=True)
            for g in range(n // LANES)]
    return jnp.concatenate(rows, axis=0)


def _ada_kernel(c_ref, w_ref, b_ref, o_ref):
    c = c_ref[...]
    c_act = c * jax.nn.sigmoid(c)
    o_ref[0] = jnp.dot(c_act, w_ref[0], precision=HIGHEST, preferred_element_type=F32) + b_ref[0]


def _ada(c, w_ada, b_ada, tn=2048):
    depth, d, n = w_ada.shape
    bsz = c.shape[0]
    return pl.pallas_call(
        _ada_kernel,
        grid=(depth, n // tn),
        in_specs=[
            pl.BlockSpec((bsz, d), lambda l, j: (0, 0)),
            pl.BlockSpec((1, d, tn), lambda l, j: (l, 0, j)),
            pl.BlockSpec((1, 1, tn), lambda l, j: (l, 0, j)),
        ],
        out_specs=pl.BlockSpec((1, bsz, tn), lambda l, j: (l, 0, j)),
        out_shape=jax.ShapeDtypeStruct((depth, bsz, n), F32),
        compiler_params=pltpu.CompilerParams(
            dimension_semantics=("arbitrary", "arbitrary"), vmem_limit_bytes=VMEM_LIMIT),
        name="ada_mod",
    )(c, w_ada, b_ada.reshape(depth, 1, n))


def _gather_rows_start(d_ref, base, ypad_ref, buf0, buf1, sem0, sem1, *, n, unroll):
    def row_copy(r, src, buf, sem):
        return pltpu.make_async_copy(
            ypad_ref.at[pl.ds(pl.multiple_of(src * SUBLANES, SUBLANES), SUBLANES)],
            buf.at[pl.ds(pl.multiple_of(r * SUBLANES, SUBLANES), SUBLANES)], sem)

    def issue(jj, carry):
        for uu in range(unroll):
            r = jj * unroll + uu
            row_copy(r, d_ref[base + r], buf0, sem0).start(priority=0)
            row_copy(r, d_ref[base + n + r], buf1, sem1).start(priority=1)
        return carry

    if unroll == n:
        issue(0, 0)
    else:
        lax.fori_loop(0, n // unroll, issue, 0)


def _gather_rows_wait(ypad_ref, buf0, buf1, sem0, sem1, *, n):
    pltpu.make_async_copy(ypad_ref.at[pl.ds(0, n * SUBLANES)], buf0, sem0).wait()
    pltpu.make_async_copy(ypad_ref.at[pl.ds(0, n * SUBLANES)], buf1, sem1).wait()


def _gathered_rows_pipeline(i, n_tiles, d_ref, ypad_ref, y0buf, y1buf, sems, *, n, unroll, consume):
    start = functools.partial(_gather_rows_start, d_ref, n=n, unroll=unroll)

    @pl.when(i == 0)
    def _():
        start(0, ypad_ref, y0buf.at[0], y1buf.at[0], sems.at[0, 0], sems.at[0, 1])

    for p in range(2):
        @pl.when(i % 2 == p)
        def _(p=p):
            _gather_rows_wait(ypad_ref, y0buf.at[p], y1buf.at[p], sems.at[p, 0], sems.at[p, 1], n=n)

            @pl.when(i + 1 < n_tiles)
            def _():
                q = 1 - p
                start((i + 1) * (2 * n), ypad_ref, y0buf.at[q], y1buf.at[q],
                      sems.at[q, 0], sems.at[q, 1])

            consume(_tiles_to_rows(y0buf.at[p], n), _tiles_to_rows(y1buf.at[p], n))


def _mixer_kernel(*refs, ts, d, fused, unroll):
    if fused:
        d_ref, x_ref, gatep_ref, modp_ref, ypad_ref = refs[:5]
        refs = refs[5:]
        xin_scr, y0buf, y1buf, sem0, sem1 = refs[-5:]
        refs = refs[:-5]
    else:
        x_ref = refs[0]
        refs = refs[1:]
    (mod_ref, g1_ref, win_ref, avg_ref, ws_ref, bsb_ref, wal_ref, bal_ref, og_ref, wout_ref, g2_ref,
     wr_ref, br_ref, xo_ref, h2_ref, gate_ref, code_ref, cnt_ref,
     st_scr, proj_scr, la_scr, o_scr, cnt_scr) = refs
    a_w = d // 2
    kw = d // 4
    vw = d // 2
    hk = kw // B_HEADS
    hv = vw // B_HEADS
    o_q, o_k, o_v, o_r, o_a = 2 * a_w, 2 * a_w + kw, 2 * a_w + 2 * kw, 2 * a_w + 2 * kw + vw, \
        2 * a_w + 2 * kw + 2 * vw

    b = pl.program_id(0)
    s = pl.program_id(1)

    @pl.when(s == 0)
    def _():
        st_scr[...] = jnp.zeros_like(st_scr)

    @pl.when(jnp.logical_and(b == 0, s == 0))
    def _():
        cnt_scr[...] = jnp.zeros_like(cnt_scr)

    if fused:
        i = b * pl.num_programs(1) + s
        n_tiles = pl.num_programs(0) * pl.num_programs(1)

        @pl.when(i == 0)
        def _():
            _gather_rows_start(d_ref, 0, ypad_ref, y0buf.at[0], y1buf.at[0], sem0.at[0], sem1.at[0],
                               n=ts, unroll=unroll)

        for p in range(2):
            @pl.when(i % 2 == p)
            def _(p=p):
                _gather_rows_wait(ypad_ref, y0buf.at[p], y1buf.at[p], sem0.at[p], sem1.at[p], n=ts)
                gate_p = gatep_ref[...]
                moe = _tiles_to_rows(y0buf.at[p], ts) * gate_p[:, 0:1] \
                    + _tiles_to_rows(y1buf.at[p], ts) * gate_p[:, 1:2]
                xin_scr[...] = x_ref[...] + modp_ref[0][5:6] * moe

        x = xin_scr[...]
        nslot = (i + 1) % 2
        _gather_rows_start(d_ref, jnp.minimum(i + 1, n_tiles - 1) * (2 * ts), ypad_ref,
                           y0buf.at[nslot], y1buf.at[nslot], sem0.at[nslot], sem1.at[nslot],
                           n=ts, unroll=ts)
    else:
        x = x_ref[...]
    mod = mod_ref[0]
    sh1, sc1, g1 = mod[0:1], mod[1:2], mod[2:3]
    sh2, sc2 = mod[3:4], mod[4:5]

    h = x * lax.rsqrt(jnp.mean(x * x, axis=-1, keepdims=True) + EPS) * (g1_ref[...] * (1.0 + sc1)) + sh1
    proj_scr[...] = jnp.dot(h.astype(BF16), win_ref[...], preferred_element_type=F32)

    u = jax.nn.gelu(proj_scr[:, 0:a_w])
    v = jax.nn.gelu(proj_scr[:, a_w:2 * a_w])
    vn = v * lax.rsqrt(jnp.mean(v * v, axis=-1, keepdims=True) + EPS) * avg_ref[...]
    vn = vn.astype(BF16)
    ahd = a_w // A_HEADS
    z_rows = []
    for ci in range(ts // A_CHUNK):
        z_heads = []
        for hd in range(A_HEADS):
            vc = vn[ci * A_CHUNK:(ci + 1) * A_CHUNK, hd * ahd:(hd + 1) * ahd]
            z_heads.append(jnp.dot(ws_ref[hd], vc, preferred_element_type=F32) + bsb_ref[hd])
        z_rows.append(jnp.concatenate(z_heads, axis=-1))
    y_a = u * jnp.concatenate(z_rows, axis=0)

    a_hi, a_lo = _split_hi_lo(proj_scr[:, o_a:o_a + LANES])
    xg2 = jnp.dot(jnp.concatenate([a_hi, a_lo], axis=-1), wal_ref[...], preferred_element_type=F32)
    xg = xg2[:, :kw] + xg2[:, kw:] + bal_ref[...]
    la_scr[...] = (jnp.minimum(xg, 0.0) - jnp.log(1.0 + jnp.exp(-jnp.abs(xg)))) / B_GATE_TAU

    c = GLA_CHUNK
    row = lax.broadcasted_iota(I32, (c, c), 0)
    col = lax.broadcasted_iota(I32, (c, c), 1)
    causal = col <= row
    tri = causal.astype(BF16)
    klane = lax.broadcasted_iota(I32, (1, kw), 1)
    bd_r = lax.broadcasted_iota(I32, (vw, kw), 0) // hv
    bd_c = lax.broadcasted_iota(I32, (vw, kw), 1) // hk
    bd_mask = (bd_r == bd_c).astype(F32)
    mid = c // 2 - 1

    st = st_scr[...]
    for j in range(ts // c):
        r0 = j * c
        g = la_scr[r0:r0 + c, :]
        q = proj_scr[r0:r0 + c, o_q:o_q + kw] * (hk ** -0.5)
        k = proj_scr[r0:r0 + c, o_k:o_k + kw]
        vv = proj_scr[r0:r0 + c, o_v:o_v + vw].astype(BF16)
        gp1 = g.astype(BF16)
        gp2, gp3 = _split_hi_lo(g - gp1.astype(F32))
        b3 = jnp.dot(tri, jnp.concatenate([gp1, gp2, gp3], axis=-1), preferred_element_type=F32)
        bcum = b3[:, :kw] + b3[:, kw:2 * kw] + b3[:, 2 * kw:]
        blast = bcum[c - 1:c, :]
        bmid = bcum[mid:mid + 1, :]
        qt = q * jnp.exp(bcum - bmid)
        kt_f = k * jnp.exp(bmid - bcum)
        qe = (qt * jnp.exp(bmid)).astype(BF16)
        kt = kt_f.astype(BF16)
        kd = (kt_f * jnp.exp(blast - bmid)).astype(BF16)
        o_inter = _nt(qe, st.astype(BF16))
        o_heads = []
        for hd in range(B_HEADS):
            m_h = jnp.logical_and(klane >= hd * hk, klane < (hd + 1) * hk)
            a_h = _nt(jnp.where(m_h, qt, 0.0).astype(BF16), kt)
            a_h = jnp.where(causal, a_h, 0.0).astype(BF16)
            o_heads.append(jnp.dot(a_h, vv[:, hd * hv:(hd + 1) * hv], preferred_element_type=F32))
        o_scr[r0:r0 + c, :] = o_inter + jnp.concatenate(o_heads, axis=-1)
        st = st * jnp.exp(blast) + _tn(vv, kd) * bd_mask
    st_scr[...] = st

    o = o_scr[...]
    og = og_ref[...]
    yb_heads = []
    for hd in range(B_HEADS):
        oh = o[:, hd * hv:(hd + 1) * hv]
        oh = oh * lax.rsqrt(jnp.mean(oh * oh, axis=-1, keepdims=True) + EPS)
        yb_heads.append(oh * og[:, hd * hv:(hd + 1) * hv])
    r = proj_scr[:, o_r:o_r + vw]
    y_b = jnp.concatenate(yb_heads, axis=-1) * (r * jax.nn.sigmoid(r))

    mixed = jnp.dot(jnp.concatenate([y_a, y_b], axis=-1).astype(BF16), wout_ref[...],
                    preferred_element_type=F32)
    xn = x + g1 * mixed
    xo_ref[...] = xn

    h2 = xn * lax.rsqrt(jnp.mean(xn * xn, axis=-1, keepdims=True) + EPS) * (g2_ref[...] * (1.0 + sc2)) + sh2
    _rows_to_tiles(h2_ref, h2, ts)

    h2h, h2l = _split_hi_lo(h2)
    lg2 = jnp.dot(jnp.concatenate([h2h, h2l], axis=-1), wr_ref[...], preferred_element_type=F32)
    logits = lg2[:, :LANES] + lg2[:, LANES:] + br_ref[...]
    lane = lax.broadcasted_iota(I32, (ts, LANES), 1).astype(F32)
    neg = -jnp.inf
    big = float(LANES)
    is_g = lane < N_GROUPS
    gl = jnp.where(is_g, logits, neg)
    gmax = jnp.max(gl, axis=-1, keepdims=True)
    grp = jnp.min(jnp.where(gl == gmax, lane, big), axis=-1, keepdims=True)
    gsum = jnp.sum(jnp.where(is_g, jnp.exp(logits - gmax), 0.0), axis=-1, keepdims=True)
    p_grp = 1.0 / gsum
    lo = N_GROUPS + EXPERTS_PER_GROUP * grp
    el = jnp.where(jnp.logical_and(lane >= lo, lane < lo + EXPERTS_PER_GROUP), logits, neg)
    t1 = jnp.max(el, axis=-1, keepdims=True)
    i1 = jnp.min(jnp.where(el == t1, lane, big), axis=-1, keepdims=True)
    el2 = jnp.where(lane == i1, neg, el)
    t2 = jnp.max(el2, axis=-1, keepdims=True)
    i2 = jnp.min(jnp.where(el2 == t2, lane, big), axis=-1, keepdims=True)
    e2 = jnp.exp(t2 - t1)
    w0 = (1.0 / (1.0 + e2)) * p_grp
    w1 = (e2 / (1.0 + e2)) * p_grp
    e0 = i1 - N_GROUPS
    e1 = i2 - N_GROUPS

    sel0 = lane == e0
    sel1 = lane == e1
    onehot = jnp.logical_or(sel0, sel1)
    rr = lax.broadcasted_iota(I32, (ts, ts), 0)
    cc = lax.broadcasted_iota(I32, (ts, ts), 1)
    before = (cc < rr).astype(BF16)
    prefix = jnp.dot(before, onehot.astype(BF16), preferred_element_type=F32) + cnt_scr[0:1, :]
    rank0 = jnp.sum(jnp.where(sel0, prefix, 0.0), axis=-1, keepdims=True)
    rank1 = jnp.sum(jnp.where(sel1, prefix, 0.0), axis=-1, keepdims=True)
    cnt_new = cnt_scr[0:1, :] + jnp.sum(onehot.astype(F32), axis=0, keepdims=True)
    cnt_scr[...] = jnp.broadcast_to(cnt_new, cnt_scr.shape)
    cnt_ref[...] = jnp.broadcast_to(cnt_new, cnt_ref.shape)

    gate_ref[...] = jnp.where(lane == 0, w0, jnp.where(lane == 1, w1, 0.0))
    code_ref[...] = jnp.concatenate(
        [_col_to_lanes(e0 * CODE_SHIFT + rank0, ts), _col_to_lanes(e1 * CODE_SHIFT + rank1, ts)],
        axis=0).astype(I32)

    if fused:
        @pl.when(i == n_tiles - 1)
        def _():
            _gather_rows_wait(ypad_ref, y0buf.at[nslot], y1buf.at[nslot], sem0.at[nslot],
                              sem1.at[nslot], n=ts)


def _mixer(l, x2, mod, g1, win, avg, ws, bsb, wal, bal, og, wout, g2, wr, br, *, bsz, seq, ts,
           moe_prev=None, unroll=8):
    t, d = x2.shape
    n_s = seq // ts
    p_in = win.shape[-1]
    kw, vw = d // 4, d // 2
    lay2 = lambda b, s, *_: (l, 0, 0)
    lay3 = lambda b, s, *_: (l, 0, 0, 0)
    tok = lambda b, s, *_: (b * n_s + s, 0)
    code_rows = 2 * ts // LANES
    fused = moe_prev is not None
    kern = functools.partial(_mixer_kernel, ts=ts, d=d, fused=fused, unroll=unroll)
    x_specs = [pl.BlockSpec((ts, d), tok)]
    x_args = [x2]
    extra_scratch = []
    prefetch = []
    if fused:
        dflat, gate_prev, ypad_prev = moe_prev
        prefetch = [dflat]
        x_specs += [
            pl.BlockSpec((ts, LANES), tok),
            pl.BlockSpec((None, 1, 6, d), lambda b, s, *_: (l - 1, b, 0, 0)),
            pl.BlockSpec(memory_space=pl.ANY),
        ]
        x_args += [gate_prev, mod, ypad_prev]
        extra_scratch = [
            pltpu.VMEM((ts, d), F32),
            pltpu.VMEM((2, ts * SUBLANES, LANES), F32),
            pltpu.VMEM((2, ts * SUBLANES, LANES), F32),
            pltpu.SemaphoreType.DMA((2,)),
            pltpu.SemaphoreType.DMA((2,)),
        ]
    grid_spec = pltpu.PrefetchScalarGridSpec(
        num_scalar_prefetch=len(prefetch),
        grid=(bsz, n_s),
        in_specs=x_specs + [
            pl.BlockSpec((None, 1, 6, d), lambda b, s, *_: (l, b, 0, 0)),
            pl.BlockSpec((None, 1, d), lay2),
            pl.BlockSpec((None, d, p_in), lay2),
            pl.BlockSpec((None, 1, d // 2), lay2),
            pl.BlockSpec((None, A_HEADS, A_CHUNK, A_CHUNK), lay3),
            pl.BlockSpec((None, A_HEADS, A_CHUNK, A_CHUNK), lay3),
            pl.BlockSpec((None, 2 * LANES, 2 * kw), lay2),
            pl.BlockSpec((None, 1, kw), lay2),
            pl.BlockSpec((None, 1, vw), lay2),
            pl.BlockSpec((None, d, d), lay2),
            pl.BlockSpec((None, 1, d), lay2),
            pl.BlockSpec((None, 2 * d, 2 * LANES), lay2),
            pl.BlockSpec((None, 1, LANES), lay2),
        ],
        out_specs=[
            pl.BlockSpec((ts, d), tok),
            pl.BlockSpec((ts * SUBLANES, LANES), tok),
            pl.BlockSpec((ts, LANES), tok),
            pl.BlockSpec((None, code_rows, LANES), lambda b, s, *_: (b * n_s + s, 0, 0)),
            pl.BlockSpec((SUBLANES, LANES), lambda b, s, *_: (0, 0)),
        ],
        scratch_shapes=[
            pltpu.VMEM((vw, kw), F32),
            pltpu.VMEM((ts, p_in), F32),
            pltpu.VMEM((ts, kw), F32),
            pltpu.VMEM((ts, vw), F32),
            pltpu.VMEM((SUBLANES, LANES), F32),
        ] + extra_scratch,
    )
    return pl.pallas_call(
        kern,
        grid_spec=grid_spec,
        out_shape=[
            jax.ShapeDtypeStruct((t, d), F32),
            jax.ShapeDtypeStruct((t * SUBLANES, LANES), F32),
            jax.ShapeDtypeStruct((t, LANES), F32),
            jax.ShapeDtypeStruct((t // ts, code_rows, LANES), I32),
            jax.ShapeDtypeStruct((SUBLANES, LANES), F32),
        ],
        compiler_params=pltpu.CompilerParams(
            dimension_semantics=("arbitrary", "arbitrary"), vmem_limit_bytes=VMEM_LIMIT),
        name="mixer",
    )(*prefetch, *x_args, mod, g1, win, avg, ws, bsb, wal, bal, og, wout, g2, wr, br)


def _slots_kernel(code_ref, cnt_ref, d_ref, be_ref, bv_ref, *, blk):
    cnt = cnt_ref[0:1, :]
    padded = jnp.floor((cnt + (blk - 1.0)) / blk) * blk
    r = lax.broadcasted_iota(I32, (LANES, LANES), 0)
    c = lax.broadcasted_iota(I32, (LANES, LANES), 1)
    upper = (r <= c).astype(F32)
    pad_end = jnp.dot(jnp.broadcast_to(padded, (SUBLANES, LANES)), upper, precision=HIGHEST,
                      preferred_element_type=F32)[0:1, :]
    pad_start = pad_end - padded

    code = code_ref[...].astype(F32)
    e = jnp.floor(code / CODE_SHIFT)
    slot = code - e * CODE_SHIFT
    for k in range(N_EXPERTS):
        slot = slot + jnp.where(e == k, pad_start[:, k:k + 1], 0.0)
    d_ref[...] = slot.astype(I32)

    shp = be_ref.shape
    jpos = (lax.broadcasted_iota(I32, shp, 0) * LANES + lax.broadcasted_iota(I32, shp, 1)).astype(F32) * blk
    be = jnp.zeros(shp, F32)
    for k in range(N_EXPERTS):
        be = be + jnp.where(pad_end[:, k:k + 1] <= jpos, 1.0, 0.0)
    be = jnp.minimum(be, N_EXPERTS - 1.0)
    seg_end = jnp.zeros(shp, F32)
    for k in range(N_EXPERTS):
        seg_end = seg_end + jnp.where(be == k, pad_start[:, k:k + 1] + cnt[:, k:k + 1], 0.0)
    be_ref[...] = be.astype(I32)
    bv_ref[...] = jnp.clip(seg_end - jpos, 0.0, float(blk)).astype(I32)


def _slots(codes2, counts, *, blk):
    kern = functools.partial(_slots_kernel, blk=blk)
    return pl.pallas_call(
        kern,
        out_shape=[
            jax.ShapeDtypeStruct(codes2.shape, I32),
            jax.ShapeDtypeStruct((SUBLANES, LANES), I32),
            jax.ShapeDtypeStruct((SUBLANES, LANES), I32),
        ],
        compiler_params=pltpu.CompilerParams(vmem_limit_bytes=VMEM_LIMIT),
        name="slots",
    )(codes2, counts)


def _dispatch_kernel(d_ref, bv_ref, h2_ref, xpad_ref, zero_scr, sem0, sem1, semz,
                     *, td, unroll, blk, n_blocks, blocks_per_step):
    i = pl.program_id(0)
    base = i * (2 * td)

    @pl.when(i == 0)
    def _():
        zero_scr[...] = jnp.zeros_like(zero_scr)

    def row_copy(r, dst, sem):
        return pltpu.make_async_copy(
            h2_ref.at[pl.ds(pl.multiple_of(r * SUBLANES, SUBLANES), SUBLANES)],
            xpad_ref.at[pl.ds(pl.multiple_of(dst * SUBLANES, SUBLANES), SUBLANES)], sem)

    def issue(jj, carry):
        for uu in range(unroll):
            r = jj * unroll + uu
            row_copy(r, d_ref[base + r], sem0).start(priority=0)
            row_copy(r, d_ref[base + td + r], sem1).start(priority=1)
        return carry

    lax.fori_loop(0, td // unroll, issue, 0)

    def pad_fill(act):
        for jb in range(blocks_per_step):
            j = i * blocks_per_step + jb
            jc = jnp.minimum(j, n_blocks - 1)
            valid = bv_ref[jc]
            n_pad = jnp.where(j < n_blocks, blk - valid, 0)
            row = jc * blk + valid
            size = blk
            while size >= 1:
                hit = (n_pad & size) != 0

                @pl.when(hit)
                def _(row=row, size=size):
                    cp = pltpu.make_async_copy(
                        zero_scr.at[pl.ds(0, size * SUBLANES)],
                        xpad_ref.at[pl.ds(pl.multiple_of(row * SUBLANES, SUBLANES), size * SUBLANES)],
                        semz)
                    act(cp)

                row = row + jnp.where(hit, size, 0)
                size //= 2

    pad_fill(lambda cp: cp.start())

    for sem in (sem0, sem1):
        pltpu.make_async_copy(h2_ref, xpad_ref.at[pl.ds(0, td * SUBLANES)], sem).wait()
    pad_fill(lambda cp: cp.wait())


def _dispatch(dflat, block_valid, h2t, *, n_blocks, blk, td, unroll=16):
    t = h2t.shape[0] // SUBLANES
    n_steps = t // td
    blocks_per_step = (n_blocks + n_steps - 1) // n_steps
    kern = functools.partial(_dispatch_kernel, td=td, unroll=unroll, blk=blk, n_blocks=n_blocks,
                             blocks_per_step=blocks_per_step)
    return pl.pallas_call(
        kern,
        grid_spec=pltpu.PrefetchScalarGridSpec(
            num_scalar_prefetch=2,
            grid=(n_steps,),
            in_specs=[pl.BlockSpec((td * SUBLANES, LANES), lambda i, dd, bv: (i, 0))],
            out_specs=pl.BlockSpec(memory_space=pl.ANY),
            scratch_shapes=[
                pltpu.VMEM((blk * SUBLANES, LANES), F32),
                pltpu.SemaphoreType.DMA, pltpu.SemaphoreType.DMA, pltpu.SemaphoreType.DMA],
        ),
        out_shape=jax.ShapeDtypeStruct((n_blocks * blk * SUBLANES, LANES), F32),
        compiler_params=pltpu.CompilerParams(
            dimension_semantics=("arbitrary",), vmem_limit_bytes=VMEM_LIMIT),
        name="dispatch",
    )(dflat, block_valid, h2t)


def _experts_kernel(be_ref, bv_ref, x_ref, w1_ref, w3_ref, w2_ref, y_ref, w1b, w3b, w2b, *, blk):
    j = pl.program_id(0)
    valid = bv_ref[j]
    used = valid > 0
    prev = be_ref[jnp.maximum(j - 1, 0)]
    fresh = jnp.logical_or(j == 0, be_ref[j] != prev)

    @pl.when(jnp.logical_and(used, fresh))
    def _():
        w1b[...] = w1_ref[...].astype(BF16)
        w3b[...] = w3_ref[...].astype(BF16)
        w2b[...] = w2_ref[...].astype(BF16)

    @pl.when(used)
    def _():
        x = _tiles_to_rows(x_ref, blk).astype(BF16)
        a = jnp.dot(x, w1b[...], preferred_element_type=F32)
        g = jnp.dot(x, w3b[...], preferred_element_type=F32)
        hmid = ((a * jax.nn.sigmoid(a)) * g).astype(BF16)
        y = jnp.dot(hmid, w2b[...], preferred_element_type=F32)
        _rows_to_tiles(y_ref, y, blk)

    @pl.when(jnp.logical_not(used))
    def _():
        y_ref[...] = jnp.zeros_like(y_ref)


def _experts(l, block_expert, block_valid, xpad, w1, w3, w2, *, n_blocks, blk):
    ne, d, de = w1.shape[1:]
    w1 = w1.reshape(-1, d, de)
    w3 = w3.reshape(-1, d, de)
    w2 = w2.reshape(-1, de, d)
    wmap = lambda j, be, bv: (l * ne + be[j], 0, 0)
    kern = functools.partial(_experts_kernel, blk=blk)
    return pl.pallas_call(
        kern,
        grid_spec=pltpu.PrefetchScalarGridSpec(
            num_scalar_prefetch=2,
            grid=(n_blocks,),
            in_specs=[
                pl.BlockSpec((blk * SUBLANES, LANES), lambda j, be, bv: (j, 0)),
                pl.BlockSpec((None, d, de), wmap),
                pl.BlockSpec((None, d, de), wmap),
                pl.BlockSpec((None, de, d), wmap),
            ],
            out_specs=pl.BlockSpec((blk * SUBLANES, LANES), lambda j, be, bv: (j, 0)),
            scratch_shapes=[
                pltpu.VMEM((d, de), BF16),
                pltpu.VMEM((d, de), BF16),
                pltpu.VMEM((de, d), BF16),
            ],
        ),
        out_shape=jax.ShapeDtypeStruct(xpad.shape, F32),
        compiler_params=pltpu.CompilerParams(
            dimension_semantics=("arbitrary",), vmem_limit_bytes=VMEM_LIMIT),
        name="experts",
    )(block_expert, block_valid, xpad, w1, w3, w2)


def _combine_kernel(d_ref, x_ref, gate_ref, mod_ref, fg_ref, ypad_ref, o_ref,
                    y0buf, y1buf, sems, *, tc, unroll):
    def consume(y0, y1):
        gate = gate_ref[...]
        xo = x_ref[...] + mod_ref[0][5:6] * (y0 * gate[:, 0:1] + y1 * gate[:, 1:2])
        o_ref[...] = xo * lax.rsqrt(jnp.mean(xo * xo, axis=-1, keepdims=True) + EPS) * fg_ref[...]

    _gathered_rows_pipeline(pl.program_id(0), pl.num_programs(0), d_ref, ypad_ref, y0buf, y1buf, sems,
                            n=tc, unroll=unroll, consume=consume)


def _combine(l, dflat, x2, gate, mod, fgain, ypad, *, seq, tc, unroll=8):
    t, d = x2.shape
    per_b = seq // tc
    kern = functools.partial(_combine_kernel, tc=tc, unroll=unroll)
    return pl.pallas_call(
        kern,
        grid_spec=pltpu.PrefetchScalarGridSpec(
            num_scalar_prefetch=1,
            grid=(t // tc,),
            in_specs=[
                pl.BlockSpec((tc, d), lambda i, dd: (i, 0)),
                pl.BlockSpec((tc, LANES), lambda i, dd: (i, 0)),
                pl.BlockSpec((None, 1, 6, d), lambda i, dd: (l, i // per_b, 0, 0)),
                pl.BlockSpec((1, d), lambda i, dd: (0, 0)),
                pl.BlockSpec(memory_space=pl.ANY),
            ],
            out_specs=pl.BlockSpec((tc, d), lambda i, dd: (i, 0)),
            scratch_shapes=[
                pltpu.VMEM((2, tc * SUBLANES, LANES), F32),
                pltpu.VMEM((2, tc * SUBLANES, LANES), F32),
                pltpu.SemaphoreType.DMA((2, 2)),
            ],
        ),
        out_shape=jax.ShapeDtypeStruct((t, d), F32),
        compiler_params=pltpu.CompilerParams(
            dimension_semantics=("arbitrary",), vmem_limit_bytes=VMEM_LIMIT),
        name="combine",
    )(dflat, x2, gate, mod, fgain, ypad)


@jax.jit
def _forward(x, c, w_ada, b_ada, norm1_gain, norm2_gain, w_in, a_v_gain, a_w_s, a_b_s, b_w_alpha,
             b_b_alpha, b_out_gain, w_out, w_group, b_group, w_router, b_router, w1, w3, w2,
             final_gain):
    bsz, seq, d = x.shape
    depth = w_ada.shape[0]
    t = bsz * seq
    ts = min(TOKEN_TILE, seq)
    blk = MOE_BLOCK
    n_blocks = (2 * t + blk - 1) // blk + N_EXPERTS
    assert t <= CODE_SHIFT and n_blocks <= SUBLANES * LANES and seq % ts == 0 and ts % LANES == 0
    p_in = w_in.shape[-1]
    p_main = p_in - B_GATE_RANK
    kw = d // 4

    mod = _ada(c, w_ada, b_ada).reshape(depth, bsz, 6, d)

    win = jnp.concatenate(
        [w_in[..., :p_main],
         jnp.pad(w_in[..., p_main:], ((0, 0), (0, 0), (0, LANES - B_GATE_RANK)))], axis=-1).astype(BF16)
    wal = _stack_hi_lo(jnp.pad(b_w_alpha, ((0, 0), (0, LANES - B_GATE_RANK), (0, 0))))
    causal = jnp.tril(jnp.ones((A_CHUNK, A_CHUNK), dtype=bool))
    ws = jnp.where(causal[None, None], a_w_s, 0.0).astype(BF16)
    bsb = jnp.broadcast_to(a_b_s[..., None], a_b_s.shape + (d // 2 // A_HEADS,))
    wout = w_out.astype(BF16)
    n_r = N_GROUPS + N_EXPERTS
    wr = _stack_hi_lo(
        jnp.pad(jnp.concatenate([w_group, w_router], axis=-1), ((0, 0), (0, 0), (0, LANES - n_r))))
    br = jnp.pad(jnp.concatenate([b_group, b_router], axis=-1), ((0, 0), (0, LANES - n_r)))

    g1 = norm1_gain.reshape(depth, 1, d)
    g2 = norm2_gain.reshape(depth, 1, d)
    avg = a_v_gain.reshape(depth, 1, d // 2)
    bal = b_b_alpha.reshape(depth, 1, kw)
    og = b_out_gain.reshape(depth, 1, d // 2)
    br = br.reshape(depth, 1, LANES)
    fg = final_gain.reshape(1, d)

    x2 = x.reshape(t, d)
    moe_prev = None
    for l in range(depth):
        x2, h2t, gate, codes, counts = _mixer(
            l, x2, mod, g1, win, avg, ws, bsb, wal, bal, og, wout, g2, wr, br,
            bsz=bsz, seq=seq, ts=ts, moe_prev=moe_prev)
        slots, block_expert, block_valid = _slots(codes.reshape(-1, LANES), counts, blk=blk)
        dflat = slots.reshape(-1)
        block_valid = block_valid.reshape(-1)
        xpad = _dispatch(dflat, block_valid, h2t, n_blocks=n_blocks, blk=blk, td=ts)
        ypad = _experts(l, block_expert.reshape(-1), block_valid, xpad, w1, w3, w2,
                        n_blocks=n_blocks, blk=blk)
        moe_prev = (dflat, gate, ypad)
    x2 = _combine(depth - 1, dflat, x2, gate, mod, fg, ypad, seq=seq, tc=ts)
    return x2.reshape(bsz, seq, d)


def kernel(x, c, w_ada, b_ada, norm1_gain, norm2_gain, w_in, a_v_gain, a_w_s, a_b_s, b_w_alpha,
           b_b_alpha, b_out_gain, w_out, w_group, b_group, w_router, b_router, w1, w3, w2,
           final_gain):
    return _forward(x, c, w_ada, b_ada, norm1_gain, norm2_gain, w_in, a_v_gain, a_w_s, a_b_s,
                    b_w_alpha, b_b_alpha, b_out_gain, w_out, w_group, b_group, w_router, b_router,
                    w1, w3, w2, final_gain)
```
